```python
import math
import jax
import jax.numpy as jnp
from jax import lax
import numpy as np

D_MODEL = 1024
BATCH = 8
SEQ = 4096
DEPTH = 2

MIX = D_MODEL
GROUP = MIX // 4
FOX_HEADS = 4
FOX_HD = GROUP // FOX_HEADS
HG_HEADS = 4
HG_DK = 128
HG_DV = GROUP // HG_HEADS
HG_CHUNK = 64
DIFF_HEADS = 4
DIFF_DV = GROUP // DIFF_HEADS
DIFF_D = DIFF_DV // 2
S5_CH = 16
S5_GROUPS = GROUP // S5_CH
S5_N = 64
ROPE_THETA = 500000.0
ROPE_FRACTION = 4
Q_BLOCK = 128
N_EXPERTS = 32
TOP_K = 4
D_FF = D_MODEL
SWIGLU_LIMIT = 7.0
SWIGLU_ALPHA = 1.702
EPS = 1e-6
COL_SIZES = (GROUP, GROUP, GROUP, FOX_HEADS,
             HG_HEADS * HG_DK, HG_HEADS * HG_DK, GROUP, GROUP,
             GROUP, GROUP, GROUP,
             GROUP)
IN_COLS = sum(COL_SIZES)

kernel_name = 'hybrid_fox_hgrn2_diff_s5_moe'


def rmsnorm(x, g):
    xf = x.astype(jnp.float32)
    y = xf * lax.rsqrt(jnp.mean(xf * xf, axis=-1, keepdims=True) + EPS)
    return (y * g.astype(jnp.float32)).astype(x.dtype)


def split_cols(p):
    offs = np.cumsum((0,) + COL_SIZES)
    return [p[..., int(offs[i]):int(offs[i + 1])] for i in range(len(COL_SIZES))]


def partial_rope(x, pos):
    dh = x.shape[-1]
    rd = dh // ROPE_FRACTION
    half = rd // 2
    inv = ROPE_THETA ** (-jnp.arange(half, dtype=jnp.float32) / half)
    ang = pos.astype(jnp.float32)[..., None] * inv
    cos = jnp.cos(ang)[:, :, None, :]
    sin = jnp.sin(ang)[:, :, None, :]
    xr = x[..., :rd].astype(jnp.float32)
    x1, x2 = xr[..., :half], xr[..., half:]
    rot = jnp.concatenate([x1 * cos - x2 * sin, x2 * cos + x1 * sin], axis=-1)
    return jnp.concatenate([rot.astype(x.dtype), x[..., rd:]], axis=-1)


def forgetting_attention(q, k, v, logf):
    B, S, H, Dh = q.shape
    nb = S // Q_BLOCK
    cum = jnp.cumsum(logf.astype(jnp.float32), axis=1).transpose(0, 2, 1)
    scale = Dh ** -0.5
    kpos = jnp.arange(S)
    qb = q.reshape(B, nb, Q_BLOCK, H, Dh).transpose(1, 0, 2, 3, 4)
    cb = cum.reshape(B, H, nb, Q_BLOCK).transpose(2, 0, 1, 3)

    def block(args):
        i, qi, ci = args
        s = jnp.einsum('bqhd,bkhd->bhqk', qi, k).astype(jnp.float32) * scale
        s = s + ci[..., :, None] - cum[:, :, None, :]
        qpos = i * Q_BLOCK + jnp.arange(Q_BLOCK)
        s = jnp.where(kpos[None, :] <= qpos[:, None], s, -jnp.inf)
        p = jax.nn.softmax(s, axis=-1).astype(v.dtype)
        return jnp.einsum('bhqk,bkhd->bqhd', p, v)

    out = lax.map(block, (jnp.arange(nb), qb, cb))
    return out.transpose(1, 0, 2, 3, 4).reshape(B, S, H, Dh)


def differential_attention(q1, q2, k1, k2, v, lam):
    B, S, H, d = q1.shape
    nb = S // Q_BLOCK
    scale = d ** -0.5
    kpos = jnp.arange(S)
    q1b = q1.reshape(B, nb, Q_BLOCK, H, d).transpose(1, 0, 2, 3, 4)
    q2b = q2.reshape(B, nb, Q_BLOCK, H, d).transpose(1, 0, 2, 3, 4)

    def block(args):
        i, a1, a2 = args
        qpos = i * Q_BLOCK + jnp.arange(Q_BLOCK)
        mask = kpos[None, :] <= qpos[:, None]
        s1 = jnp.einsum('bqhd,bkhd->bhqk', a1, k1).astype(jnp.float32) * scale
        s2 = jnp.einsum('bqhd,bkhd->bhqk', a2, k2).astype(jnp.float32) * scale
        p = (jax.nn.softmax(jnp.where(mask, s1, -jnp.inf), axis=-1)
             - lam * jax.nn.softmax(jnp.where(mask, s2, -jnp.inf), axis=-1))
        return jnp.einsum('bhqk,bkhd->bqhd', p.astype(v.dtype), v)

    out = lax.map(block, (jnp.arange(nb), q1b, q2b))
    return out.transpose(1, 0, 2, 3, 4).reshape(B, S, H, v.shape[-1])


def hgrn2_recurrence(q, logf, k, v):
    B, S, H, K = q.shape
    V = v.shape[-1]
    C = HG_CHUNK
    nc = S // C

    def to_chunks(a):
        return a.reshape(B, nc, C, H, a.shape[-1]).transpose(1, 0, 3, 2, 4)

    causal = jnp.tril(jnp.ones((C, C), dtype=bool))[:, :, None]

    def step(state, inp):
        qi, gi, ki, vi = inp
        G = jnp.cumsum(gi, axis=2)
        diff = G[:, :, :, None, :] - G[:, :, None, :, :]
        decay = jnp.exp(jnp.where(causal, diff, -jnp.inf))
        A = jnp.einsum('bhtk,bhsk,bhtsk->bhts', qi, ki, decay)
        o = (jnp.einsum('bhts,bhsv->bhtv', A, vi)
             + jnp.einsum('bhtk,bhkv->bhtv', qi * jnp.exp(G), state))
        g_last = G[:, :, -1]
        kdec = ki * jnp.exp(g_last[:, :, None, :] - G)
        new_state = jnp.exp(g_last)[..., None] * state + jnp.einsum('bhsk,bhsv->bhkv', kdec, vi)
        return new_state, o

    s0 = jnp.zeros((B, H, K, V), jnp.float32)
    _, o = lax.scan(step, s0, (to_chunks(q), to_chunks(logf), to_chunks(k), to_chunks(v)))
    return o.transpose(1, 0, 3, 2, 4).reshape(B, S, H, V)


def s5_ssm(u, a_re, a_im, log_step, b_re, b_im, c_re, c_im, d_skip):
    uf = u.astype(jnp.float32)
    lr = jnp.minimum(a_re.astype(jnp.float32), -1e-4)
    li = a_im.astype(jnp.float32)
    dt = jnp.exp(log_step.astype(jnp.float32))[:, None]
    mag = jnp.exp(lr * dt)
    ar = mag * jnp.cos(li * dt)
    ai = mag * jnp.sin(li * dt)
    den = lr * lr + li * li
    zr = ((ar - 1.0) * lr + ai * li) / den
    zi = (ai * lr - (ar - 1.0) * li) / den
    br = b_re.astype(jnp.float32)
    bi = b_im.astype(jnp.float32)
    bbr = zr[..., None] * br - zi[..., None] * bi
    bbi = zr[..., None] * bi + zi[..., None] * br
    bur = jnp.einsum('bsgh,gnh->bsgn', uf, bbr)
    bui = jnp.einsum('bsgh,gnh->bsgn', uf, bbi)
    arr = jnp.broadcast_to(ar, bur.shape)
    aii = jnp.broadcast_to(ai, bur.shape)

    def combine(e1, e2):
        a1r, a1i, b1r, b1i = e1
        a2r, a2i, b2r, b2i = e2
        return (a2r * a1r - a2i * a1i,
                a2r * a1i + a2i * a1r,
                a2r * b1r - a2i * b1i + b2r,
                a2r * b1i + a2i * b1r + b2i)

    _, _, xr, xi = lax.associative_scan(combine, (arr, aii, bur, bui), axis=1)
    y = (jnp.einsum('bsgn,ghn->bsgh', xr, c_re.astype(jnp.float32))
         - jnp.einsum('bsgn,ghn->bsgh', xi, c_im.astype(jnp.float32))
         + d_skip.astype(jnp.float32) * uf)
    return y


def token_mixer(h, pos, layer, lb, w_in, w_out, fox_fb, hg_norm_g, lam_q1, lam_k1, lam_q2, lam_k2,
                diff_subln_g, a_re, a_im, log_step, b_re, b_im, c_re, c_im, d_skip, glu_w):
    B, S, _ = h.shape
    fq, fk, fv, ff, hq, hf, hi, hg, dq, dk, dv, su = split_cols(h @ w_in)
    fox_logf = jax.nn.log_sigmoid((ff + fox_fb).astype(jnp.float32))
    y_fox = forgetting_attention(fq.reshape(B, S, FOX_HEADS, FOX_HD),
                                 fk.reshape(B, S, FOX_HEADS, FOX_HD),
                                 fv.reshape(B, S, FOX_HEADS, FOX_HD), fox_logf)
    y_fox = y_fox.reshape(B, S, GROUP).astype(jnp.float32)
    lbh = lb.reshape(HG_HEADS, HG_DK)
    f = lbh + (1.0 - lbh) * jax.nn.sigmoid(hf.reshape(B, S, HG_HEADS, HG_DK).astype(jnp.float32))
    y_hg = hgrn2_recurrence(jax.nn.silu(hq.reshape(B, S, HG_HEADS, HG_DK).astype(jnp.float32)),
                            jnp.log(f), 1.0 - f,
                            hi.reshape(B, S, HG_HEADS, HG_DV).astype(jnp.float32))
    y_hg = rmsnorm(y_hg, hg_norm_g).reshape(B, S, GROUP) * jax.nn.silu(hg.astype(jnp.float32))
    q = partial_rope(dq.reshape(B, S, 2 * DIFF_HEADS, DIFF_D), pos).reshape(B, S, DIFF_HEADS, 2, DIFF_D)
    k = partial_rope(dk.reshape(B, S, 2 * DIFF_HEADS, DIFF_D), pos).reshape(B, S, DIFF_HEADS, 2, DIFF_D)
    lam_init = 0.8 - 0.6 * math.exp(-0.3 * layer)
    lam = (jnp.exp(jnp.sum(lam_q1.astype(jnp.float32) * lam_k1.astype(jnp.float32)))
           - jnp.exp(jnp.sum(lam_q2.astype(jnp.float32) * lam_k2.astype(jnp.float32))) + lam_init)
    y_diff = differential_attention(q[..., 0, :], q[..., 1, :], k[..., 0, :], k[..., 1, :],
                                    dv.reshape(B, S, DIFF_HEADS, DIFF_DV), lam)
    y_diff = (rmsnorm(y_diff.astype(jnp.float32), diff_subln_g) * (1.0 - lam_init)).reshape(B, S, GROUP)
    y_s5 = s5_ssm(su.reshape(B, S, S5_GROUPS, S5_CH), a_re, a_im, log_step, b_re, b_im, c_re, c_im, d_skip)
    z = jax.nn.gelu(y_s5.reshape(B, S, GROUP)).astype(h.dtype) @ glu_w
    y_s5 = (z[..., :GROUP] * jax.nn.sigmoid(z[..., GROUP:])).astype(jnp.float32)
    y = jnp.concatenate([y_fox, y_hg, y_diff, y_s5], axis=-1).astype(h.dtype)
    return y @ w_out


def moe_ffn(h, router_w, router_b, w1, b1, w2, b2):
    B, S, D = h.shape
    t = h.reshape(B * S, D)
    logits = (t @ router_w + router_b).astype(jnp.float32)
    vals, idx = lax.top_k(logits, TOP_K)
    wts = jax.nn.softmax(vals, axis=-1)
    gates = jnp.einsum('tk,tke->te', wts, jax.nn.one_hot(idx, N_EXPERTS, dtype=jnp.float32))
    out = jnp.zeros((B * S, D), jnp.float32)
    for e in range(N_EXPERTS):
        hh = t @ w1[e] + b1[e]
        glu = jnp.minimum(hh[..., ::2], SWIGLU_LIMIT)
        lin = jnp.clip(hh[..., 1::2], -SWIGLU_LIMIT, SWIGLU_LIMIT)
        act = glu * jax.nn.sigmoid(SWIGLU_ALPHA * glu) * (lin + 1.0)
        out = out + gates[:, e:e + 1] * (act @ w2[e] + b2[e]).astype(jnp.float32)
    return out.astype(h.dtype).reshape(B, S, D)


def setup_inputs(seed: int = 0) -> dict:
    key = jax.random.key(seed)
    ks = iter(jax.random.split(key, 48))
    f32 = jnp.float32

    def nrm(shape, s):
        return s * jax.random.normal(next(ks), shape, f32)

    x = nrm((BATCH, SEQ, D_MODEL), 1.0)
    c = nrm((BATCH, D_MODEL), 1.0)
    positions = (jax.random.randint(next(ks), (BATCH, 1), 0, SEQ, dtype=jnp.int32)
                 + jnp.arange(SEQ, dtype=jnp.int32)[None, :])
    return {
        'x': x,
        'c': c,
        'positions': positions,
        'ada_w': nrm((DEPTH, D_MODEL, 6 * D_MODEL), 0.5 * D_MODEL ** -0.5),
        'ada_b': nrm((DEPTH, 6 * D_MODEL), 0.02),
        'pre_norm_g': 1.0 + nrm((DEPTH, 2, D_MODEL), 0.02),
        'post_norm_g': 1.0 + nrm((DEPTH, 2, D_MODEL), 0.02),
        'w_in': nrm((DEPTH, D_MODEL, IN_COLS), D_MODEL ** -0.5),
        'w_out': nrm((DEPTH, MIX, D_MODEL), MIX ** -0.5),
        'fox_fb': 2.0 + nrm((DEPTH, FOX_HEADS), 0.1),
        'hg_lower': nrm((DEPTH, HG_HEADS * HG_DK), 0.1),
        'hg_norm_g': 1.0 + nrm((DEPTH, HG_DV), 0.02),
        'diff_lam_q1': nrm((DEPTH, DIFF_D), 0.1),
        'diff_lam_k1': nrm((DEPTH, DIFF_D), 0.1),
        'diff_lam_q2': nrm((DEPTH, DIFF_D), 0.1),
        'diff_lam_k2': nrm((DEPTH, DIFF_D), 0.1),
        'diff_subln_g': 1.0 + nrm((DEPTH, DIFF_DV), 0.02),
        's5_a_re': -0.5 + nrm((DEPTH, S5_GROUPS, S5_N), 0.01),
        's5_a_im': math.pi * jnp.arange(S5_N, dtype=f32) + nrm((DEPTH, S5_GROUPS, S5_N), 0.01),
        's5_log_step': jax.random.uniform(next(ks), (DEPTH, S5_GROUPS), f32,
                                          math.log(1e-3), math.log(1e-1)),
        's5_b_re': nrm((DEPTH, S5_GROUPS, S5_N, S5_CH), (2 * S5_CH) ** -0.5),
        's5_b_im': nrm((DEPTH, S5_GROUPS, S5_N, S5_CH), (2 * S5_CH) ** -0.5),
        's5_c_re': nrm((DEPTH, S5_GROUPS, S5_CH, S5_N), S5_N ** -0.5),
        's5_c_im': nrm((DEPTH, S5_GROUPS, S5_CH, S5_N), S5_N ** -0.5),
        's5_d': nrm((DEPTH, S5_GROUPS, S5_CH), 1.0),
        's5_glu_w': nrm((DEPTH, GROUP, 2 * GROUP), GROUP ** -0.5),
        'router_w': nrm((DEPTH, D_MODEL, N_EXPERTS), D_MODEL ** -0.5),
        'router_b': nrm((DEPTH, N_EXPERTS), 0.01),
        'exp_w1': nrm((DEPTH, N_EXPERTS, D_MODEL, 2 * D_FF), D_MODEL ** -0.5),
        'exp_b1': nrm((DEPTH, N_EXPERTS, 2 * D_FF), 0.01),
        'exp_w2': nrm((DEPTH, N_EXPERTS, D_FF, D_MODEL), D_FF ** -0.5),
        'exp_b2': nrm((DEPTH, N_EXPERTS, D_MODEL), 0.01),
    }


def reference(x, c, positions, ada_w, ada_b, pre_norm_g, post_norm_g, w_in, w_out, fox_fb, hg_lower,
              hg_norm_g, diff_lam_q1, diff_lam_k1, diff_lam_q2, diff_lam_k2, diff_subln_g, s5_a_re,
              s5_a_im, s5_log_step, s5_b_re, s5_b_im, s5_c_re, s5_c_im, s5_d, s5_glu_w, router_w,
              router_b, exp_w1, exp_b1, exp_w2, exp_b2):
    lb_all = jnp.cumsum(jax.nn.softmax(hg_lower.astype(jnp.float32), axis=0), axis=0)
    lb_all = lb_all - lb_all[0:1]
    cs = jax.nn.silu(c)
    for l in range(DEPTH):
        mod = cs @ ada_w[l] + ada_b[l]
        sh1, sc1, g1, sh2, sc2, g2 = jnp.split(mod, 6, axis=-1)
        h = rmsnorm(x, pre_norm_g[l, 0]) * (1.0 + sc1[:, None, :]) + sh1[:, None, :]
        y = token_mixer(h, positions, l, lb_all[l], w_in[l], w_out[l], fox_fb[l], hg_norm_g[l],
                        diff_lam_q1[l], diff_lam_k1[l], diff_lam_q2[l], diff_lam_k2[l], diff_subln_g[l],
                        s5_a_re[l], s5_a_im[l], s5_log_step[l], s5_b_re[l], s5_b_im[l], s5_c_re[l],
                        s5_c_im[l], s5_d[l], s5_glu_w[l])
        x = x + g1[:, None, :] * rmsnorm(y, post_norm_g[l, 0])
        h = rmsnorm(x, pre_norm_g[l, 1]) * (1.0 + sc2[:, None, :]) + sh2[:, None, :]
        y = moe_ffn(h, router_w[l], router_b[l], exp_w1[l], exp_b1[l], exp_w2[l], exp_b2[l])
        x = x + g2[:, None, :] * rmsnorm(y, post_norm_g[l, 1])
    return x
```

```python
import functools
import math

import numpy as np
import jax
import jax.numpy as jnp
from jax import lax
from jax.experimental import pallas as pl
from jax.experimental.pallas import tpu as pltpu

F32 = jnp.float32
BF16 = jnp.bfloat16

D_MODEL = 1024
GROUP = 256
FOX_HEADS = 4
FOX_HD = 64
HG_HEADS = 4
HG_DK = 128
HG_DV = 64
DIFF_HEADS = 4
DIFF_DV = 64
DIFF_D = 32
S5_CH = 16
S5_GROUPS = 16
S5_N = 64
ROPE_THETA = 500000.0
N_EXPERTS = 32
TOP_K = 4
D_FF = 1024
SWIGLU_LIMIT = 7.0
SWIGLU_ALPHA = 1.702
EPS = 1e-6
NEG_BIG = -1e30

LANES = 128
VMEM_LIMIT = 56 * 1024 * 1024

ROW_TILE = 512
ATT_TQ = 256
ATT_TK = 256
HG_CHUNK = 64
S5_STEPS = 128
EXP_TILE = 256
CMB_TILE = 256

HG_COLS = 2 * HG_HEADS * HG_DK + 2 * GROUP
MAIN_COLS = 3 * GROUP + HG_COLS + 3 * GROUP + GROUP


def _cparams(sem):
    return pltpu.CompilerParams(dimension_semantics=sem, vmem_limit_bytes=VMEM_LIMIT)


def _sigmoid(x):
    return 1.0 / (1.0 + jnp.exp(-x))


def _shr(x, pow2):
    return lax.shift_right_logical(x, int(math.log2(pow2)))


def _dot(a, b):
    return jnp.dot(a, b, preferred_element_type=F32)


def _dot_nt(a, b):
    return lax.dot_general(a, b, (((1,), (1,)), ((), ())), preferred_element_type=F32)


def _dot_tn(a, b):
    return lax.dot_general(a, b, (((0,), (0,)), ((), ())), preferred_element_type=F32)


def _split_dot(m_bf16, x):
    hi = x.astype(BF16)
    r1 = x - hi.astype(F32)
    mid = r1.astype(BF16)
    lo = (r1 - mid.astype(F32)).astype(BF16)
    return _dot(m_bf16, hi) + _dot(m_bf16, mid) + _dot(m_bf16, lo)


def _ada_kernel(c_ref, w_ref, b_ref, o_ref):
    c = c_ref[...]
    cs = c * _sigmoid(c)
    o_ref[0] = jnp.dot(cs, w_ref[0], preferred_element_type=F32,
                       precision=lax.Precision.HIGHEST) + b_ref[0]


def _ada_call(c, ada_w, ada_b):
    depth, d, n = ada_w.shape
    b = c.shape[0]
    tn = 1536
    return pl.pallas_call(
        _ada_kernel,
        grid=(depth, n // tn),
        in_specs=[pl.BlockSpec((b, d), lambda l, j: (0, 0)),
                  pl.BlockSpec((1, d, tn), lambda l, j: (l, 0, j)),
                  pl.BlockSpec((1, 1, tn), lambda l, j: (l, 0, j))],
        out_specs=pl.BlockSpec((1, b, tn), lambda l, j: (l, 0, j)),
        out_shape=jax.ShapeDtypeStruct((depth, b, n), F32),
        compiler_params=_cparams(("arbitrary", "arbitrary")),
    )(c, ada_w, ada_b.reshape(depth, 1, n))


def _inproj_kernel(x_ref, g_ref, sc_ref, sh_ref, w_ref, wf_ref, fb_ref, pos_ref, inv_ref,
                   ma_ref, mb_ref,
                   fox_ref, fcol_ref, frow_ref, hg_ref, diff_ref, s5_ref, carry_ref):
    i = pl.program_id(1)
    tm = x_ref.shape[0]

    @pl.when(i == 0)
    def _():
        carry_ref[...] = jnp.zeros_like(carry_ref)

    x = x_ref[...]
    ms = jnp.mean(x * x, axis=-1, keepdims=True)
    h = (x * lax.rsqrt(ms + EPS) * g_ref[...]) * (1.0 + sc_ref[0]) + sh_ref[0]
    hb = h.astype(BF16)

    o = 0
    pf = _dot(hb, w_ref[:, o:o + 3 * GROUP])
    fox_ref[:, 0:GROUP] = (pf[:, 0:GROUP] * (FOX_HD ** -0.5)).astype(BF16)
    fox_ref[:, GROUP:3 * GROUP] = pf[:, GROUP:3 * GROUP].astype(BF16)
    o += 3 * GROUP

    ff = _dot(hb, wf_ref[...]) + fb_ref[...]
    logf = jnp.minimum(ff, 0.0) - jnp.log(1.0 + jnp.exp(-jnp.abs(ff)))
    r = lax.broadcasted_iota(jnp.int32, (tm, tm), 0)
    cidx = lax.broadcasted_iota(jnp.int32, (tm, tm), 1)
    tri = jnp.where(cidx <= r, 1.0, 0.0).astype(BF16)
    cum = _split_dot(tri, logf) + carry_ref[...]
    carry_ref[...] = cum[tm - 1:tm, :]
    fcol_ref[...] = cum
    cum_t = jnp.transpose(cum)
    nsub = tm // ATT_TK
    for j in range(nsub):
        frow_ref[0, j] = cum_t[0:8, j * ATT_TK:(j + 1) * ATT_TK]

    hg_ref[...] = _dot(hb, w_ref[:, o:o + HG_COLS]).astype(BF16)
    o += HG_COLS

    pd = _dot(hb, w_ref[:, o:o + 3 * GROUP])
    o += 3 * GROUP
    pos = pos_ref[...].astype(F32)
    ang = pos * inv_ref[...]
    cos = jnp.cos(ang)
    sin = jnp.sin(ang)
    ma = ma_ref[...]
    mb = mb_ref[...]

    def rope(xx):
        outs = []
        for j in range(GROUP // LANES):
            sl = slice(j * LANES, (j + 1) * LANES)
            xs = xx[:, sl]
            partner = (pltpu.roll(xs, LANES - 4, axis=1) * ma[:, sl]
                       + pltpu.roll(xs, 4, axis=1) * mb[:, sl])
            outs.append(xs * cos[:, sl] + partner * sin[:, sl])
        return jnp.concatenate(outs, axis=1)

    diff_ref[:, 0:GROUP] = (rope(pd[:, 0:GROUP]) * (DIFF_D ** -0.5)).astype(BF16)
    diff_ref[:, GROUP:2 * GROUP] = rope(pd[:, GROUP:2 * GROUP]).astype(BF16)
    diff_ref[:, 2 * GROUP:3 * GROUP] = pd[:, 2 * GROUP:3 * GROUP].astype(BF16)

    s5_ref[...] = _dot(hb, w_ref[:, o:o + GROUP]).astype(BF16)


def _inproj_call(x2, g_pre, sc, sh, w_main, w_f, fb, pos, inv_lane, mask_a, mask_b, batch, seq):
    t = x2.shape[0]
    tm = min(ROW_TILE, seq)
    nt = seq // tm
    nsub = tm // ATT_TK
    row = lambda b, i: (b * nt + i, 0)
    const2 = lambda b, i: (0, 0)
    per_b = lambda b, i: (b, 0, 0)
    outs = pl.pallas_call(
        _inproj_kernel,
        grid=(batch, nt),
        in_specs=[pl.BlockSpec((tm, D_MODEL), row),
                  pl.BlockSpec((1, D_MODEL), const2),
                  pl.BlockSpec((1, 1, D_MODEL), per_b),
                  pl.BlockSpec((1, 1, D_MODEL), per_b),
                  pl.BlockSpec((D_MODEL, MAIN_COLS), const2),
                  pl.BlockSpec((D_MODEL, LANES), const2),
                  pl.BlockSpec((1, LANES), const2),
                  pl.BlockSpec((tm, 1), row),
                  pl.BlockSpec((1, GROUP), const2),
                  pl.BlockSpec((1, GROUP), const2),
                  pl.BlockSpec((1, GROUP), const2)],
        out_specs=[pl.BlockSpec((tm, 3 * GROUP), row),
                   pl.BlockSpec((tm, LANES), row),
                   pl.BlockSpec((1, nsub, 8, ATT_TK), lambda b, i: (b, i, 0, 0)),
                   pl.BlockSpec((tm, HG_COLS), row),
                   pl.BlockSpec((tm, 3 * GROUP), row),
                   pl.BlockSpec((tm, GROUP), row)],
        out_shape=[jax.ShapeDtypeStruct((t, 3 * GROUP), BF16),
                   jax.ShapeDtypeStruct((t, LANES), F32),
                   jax.ShapeDtypeStruct((batch, seq // ATT_TK, 8, ATT_TK), F32),
                   jax.ShapeDtypeStruct((t, HG_COLS), BF16),
                   jax.ShapeDtypeStruct((t, 3 * GROUP), BF16),
                   jax.ShapeDtypeStruct((t, GROUP), BF16)],
        scratch_shapes=[pltpu.VMEM((1, LANES), F32)],
        compiler_params=_cparams(("arbitrary", "arbitrary")),
    )(x2, g_pre, sc, sh, w_main, w_f, fb, pos, inv_lane, mask_a, mask_b)
    return outs


def _softmax_stream(q, k_ref, v_ref, kcol, vcol, dk, dv, n_full, qi, bias_fn):
    tq = q.shape[0]
    tk = ATT_TK

    def block(kb, carry, masked):
        m, l, acc = carry
        start = pl.multiple_of(kb * tk, tk)
        kblk = k_ref[pl.ds(start, tk), kcol:kcol + dk]
        vblk = v_ref[pl.ds(start, tk), vcol:vcol + dv]
        s = _dot_nt(q, kblk)
        bias = bias_fn(kb)
        if bias is not None:
            s = s + bias
        if masked:
            rr = lax.broadcasted_iota(jnp.int32, (tq, tk), 0)
            cc = lax.broadcasted_iota(jnp.int32, (tq, tk), 1)
            s = jnp.where(cc <= rr, s, NEG_BIG)
        m_new = jnp.maximum(m, jnp.max(s, axis=-1, keepdims=True))
        alpha = jnp.exp(m - m_new)
        p = jnp.exp(s - m_new)
        l = alpha * l + jnp.sum(p, axis=-1, keepdims=True)
        acc = alpha * acc + _dot(p.astype(BF16), vblk)
        return m_new, l, acc

    init = (jnp.full((tq, 1), NEG_BIG, F32), jnp.zeros((tq, 1), F32), jnp.zeros((tq, dv), F32))
    carry = lax.fori_loop(0, n_full, lambda kb, c: block(kb, c, False), init)
    _, l, acc = block(qi, carry, True)
    return acc, l


def _fox_kernel(q_ref, k_ref, v_ref, fcol_ref, frow_ref, o_ref):
    qi = pl.program_id(1)
    outs = []
    for h in range(FOX_HEADS):
        q = q_ref[:, h * FOX_HD:(h + 1) * FOX_HD]
        fq = fcol_ref[:, h:h + 1]

        def bias_fn(kb, fq=fq, h=h):
            return fq - frow_ref[0, kb, h:h + 1, :]

        acc, l = _softmax_stream(q, k_ref, v_ref, h * FOX_HD, h * FOX_HD, FOX_HD, FOX_HD,
                                 qi, qi, bias_fn)
        outs.append(acc / l)
    o_ref[...] = jnp.concatenate(outs, axis=1).astype(BF16)


def _fox_call(fox, fcol, frow, batch, seq):
    t = fox.shape[0]
    nq = seq // ATT_TQ
    nk = seq // ATT_TK
    return pl.pallas_call(
        _fox_kernel,
        grid=(batch, nq),
        in_specs=[pl.BlockSpec((ATT_TQ, GROUP), lambda b, i: (b * nq + i, 0)),
                  pl.BlockSpec((seq, GROUP), lambda b, i: (b, 1)),
                  pl.BlockSpec((seq, GROUP), lambda b, i: (b, 2)),
                  pl.BlockSpec((ATT_TQ, LANES), lambda b, i: (b * nq + i, 0)),
                  pl.BlockSpec((1, nk, 8, ATT_TK), lambda b, i: (b, 0, 0, 0))],
        out_specs=pl.BlockSpec((ATT_TQ, GROUP), lambda b, i: (b * nq + i, 0)),
        out_shape=jax.ShapeDtypeStruct((t, GROUP), BF16),
        compiler_params=_cparams(("arbitrary", "arbitrary")),
    )(fox, fox, fox, fcol, frow)


def _diff_kernel(lam_ref, q_ref, k_ref, v_ref, g_ref, o_ref, *, out_scale):
    qi = pl.program_id(1)
    lam = lam_ref[0]
    outs = []
    for h in range(DIFF_HEADS):
        c0 = h * DIFF_DV
        y = None
        for comp in range(2):
            q = q_ref[:, c0 + comp * DIFF_D:c0 + (comp + 1) * DIFF_D]
            acc, l = _softmax_stream(q, k_ref, v_ref, c0 + comp * DIFF_D, c0, DIFF_D, DIFF_DV,
                                     qi, qi, lambda kb: None)
            y = acc / l if comp == 0 else y - lam * (acc / l)
        ms = jnp.mean(y * y, axis=-1, keepdims=True)
        outs.append(y * lax.rsqrt(ms + EPS) * g_ref[...] * out_scale)
    o_ref[...] = jnp.concatenate(outs, axis=1).astype(BF16)


def _diff_call(diff, lam, subln_g, out_scale, batch, seq):
    t = diff.shape[0]
    nq = seq // ATT_TQ
    return pl.pallas_call(
        functools.partial(_diff_kernel, out_scale=out_scale),
        grid=(batch, nq),
        in_specs=[pl.BlockSpec(memory_space=pltpu.SMEM),
                  pl.BlockSpec((ATT_TQ, GROUP), lambda b, i: (b * nq + i, 0)),
                  pl.BlockSpec((seq, GROUP), lambda b, i: (b, 1)),
                  pl.BlockSpec((seq, GROUP), lambda b, i: (b, 2)),
                  pl.BlockSpec((1, DIFF_DV), lambda b, i: (0, 0))],
        out_specs=pl.BlockSpec((ATT_TQ, GROUP), lambda b, i: (b * nq + i, 0)),
        out_shape=jax.ShapeDtypeStruct((t, GROUP), BF16),
        compiler_params=_cparams(("arbitrary", "arbitrary")),
    )(lam, diff, diff, diff, subln_g)


_HG_LEVELS = (32, 16, 8)


def _hg_ref_rows(G, level):
    C = G.shape[0]
    rows = []
    for tile in range(C // 8):
        if level is None:
            r = tile * 8 + 3
        else:
            grp = (tile * 8) // (2 * level)
            r = grp * 2 * level + level - 1
        rows.append(jnp.broadcast_to(G[r:r + 1, :], (8, G.shape[1])))
    return jnp.concatenate(rows, axis=0)


def _hgrn2_kernel(p_ref, lb_ref, ng_ref, o_ref, q_s, g_s, k_s, state_ref):
    i = pl.program_id(1)
    tm = p_ref.shape[0]
    C = HG_CHUNK
    KW = HG_HEADS * HG_DK

    @pl.when(i == 0)
    def _():
        state_ref[...] = jnp.zeros_like(state_ref)

    hq = p_ref[:, 0:KW].astype(F32)
    q_s[...] = hq * _sigmoid(hq)
    lb = lb_ref[...]
    f = lb + (1.0 - lb) * _sigmoid(p_ref[:, KW:2 * KW].astype(F32))
    k_s[...] = 1.0 - f
    r = lax.broadcasted_iota(jnp.int32, (tm, tm), 0)
    c = lax.broadcasted_iota(jnp.int32, (tm, tm), 1)
    bd = jnp.where((c <= r) & (_shr(r, C) == _shr(c, C)), 1.0, 0.0).astype(BF16)
    g_s[...] = _split_dot(bd, jnp.log(f))

    rr = lax.broadcasted_iota(jnp.int32, (C, C), 0)
    cc = lax.broadcasted_iota(jnp.int32, (C, C), 1)
    level_masks = []
    for m in _HG_LEVELS:
        same_grp = _shr(rr, 2 * m) == _shr(cc, 2 * m)
        level_masks.append(same_grp & ((rr & (2 * m - 1)) >= m) & ((cc & (2 * m - 1)) < m))
    diag_mask = (_shr(rr, 8) == _shr(cc, 8)) & (cc <= rr)
    row_idx = lax.broadcasted_iota(jnp.int32, (C, HG_DK), 0)

    def chunk(ci, _):
        base = pl.multiple_of(ci * C, C)
        for h in range(HG_HEADS):
            ks = slice(h * HG_DK, (h + 1) * HG_DK)
            q = q_s[pl.ds(base, C), ks]
            k = k_s[pl.ds(base, C), ks]
            G = g_s[pl.ds(base, C), ks]
            v = p_ref[pl.ds(base, C), 2 * KW + h * HG_DV:2 * KW + (h + 1) * HG_DV]
            st = state_ref[h]
            gl = G[C - 1:C, :]
            o = _dot_nt((q * jnp.exp(G)).astype(BF16), st.astype(BF16))
            a = jnp.zeros((C, C), F32)
            for m, mask in zip(_HG_LEVELS, level_masks):
                ref = _hg_ref_rows(G, m)
                upper = (row_idx & (2 * m - 1)) >= m
                e = jnp.exp(jnp.where(upper, G - ref, ref - G))
                am = _dot_nt((q * e).astype(BF16), (k * e).astype(BF16))
                a = a + jnp.where(mask, am, 0.0)
            ref = _hg_ref_rows(G, None)
            d = jnp.clip(G - ref, -80.0, 80.0)
            am = _dot_nt((q * jnp.exp(d)).astype(BF16), (k * jnp.exp(-d)).astype(BF16))
            a = a + jnp.where(diag_mask, am, 0.0)
            o = o + _dot(a.astype(BF16), v)
            kd = (k * jnp.exp(gl - G)).astype(BF16)
            state_ref[h] = jnp.exp(gl) * st + _dot_tn(v, kd)
            ms = jnp.mean(o * o, axis=-1, keepdims=True)
            on = o * lax.rsqrt(ms + EPS) * ng_ref[...]
            gate = p_ref[pl.ds(base, C), 2 * KW + GROUP + h * HG_DV:
                         2 * KW + GROUP + (h + 1) * HG_DV].astype(F32)
            o_ref[pl.ds(base, C), h * HG_DV:(h + 1) * HG_DV] = (
                on * (gate * _sigmoid(gate))).astype(BF16)
        return 0

    lax.fori_loop(0, tm // C, chunk, 0)


def _hgrn2_call(hg, lb, norm_g, batch, seq):
    t = hg.shape[0]
    tm = min(ROW_TILE, seq)
    nt = seq // tm
    kw = HG_HEADS * HG_DK
    return pl.pallas_call(
        _hgrn2_kernel,
        grid=(batch, nt),
        in_specs=[pl.BlockSpec((tm, HG_COLS), lambda b, i: (b * nt + i, 0)),
                  pl.BlockSpec((1, kw), lambda b, i: (0, 0)),
                  pl.BlockSpec((1, HG_DV), lambda b, i: (0, 0))],
        out_specs=pl.BlockSpec((tm, GROUP), lambda b, i: (b * nt + i, 0)),
        out_shape=jax.ShapeDtypeStruct((t, GROUP), BF16),
        scratch_shapes=[pltpu.VMEM((tm, kw), F32), pltpu.VMEM((tm, kw), F32),
                        pltpu.VMEM((tm, kw), F32),
                        pltpu.VMEM((HG_HEADS, HG_DV, HG_DK), F32)],
        compiler_params=_cparams(("arbitrary", "arbitrary")),
    )(hg, lb, norm_g)


def _s5_kernel(u_ref, bb_ref, ar_ref, ai_ref, cm_ref, d_ref, glu_ref, o_ref, bu_s, xr_s, xi_s,
               *, batch):
    i = pl.program_id(0)
    nst = S5_GROUPS * S5_N
    steps = u_ref.shape[0] // batch

    @pl.when(i == 0)
    def _():
        xr_s[...] = jnp.zeros_like(xr_s)
        xi_s[...] = jnp.zeros_like(xi_s)

    u = u_ref[...]
    bu_s[...] = _dot(u, bb_ref[...])
    ar = jnp.broadcast_to(ar_ref[...], (batch, nst))
    ai = jnp.broadcast_to(ai_ref[...], (batch, nst))

    def step(t, carry):
        xr, xi = carry
        r0 = pl.multiple_of(t * batch, batch)
        nr = ar * xr - ai * xi + bu_s[pl.ds(r0, batch), 0:nst]
        ni = ar * xi + ai * xr + bu_s[pl.ds(r0, batch), nst:2 * nst]
        bu_s[pl.ds(r0, batch), 0:nst] = nr
        bu_s[pl.ds(r0, batch), nst:2 * nst] = ni
        return nr, ni

    xr, xi = lax.fori_loop(0, steps, step, (xr_s[...], xi_s[...]))
    xr_s[...] = xr
    xi_s[...] = xi
    y = _dot(bu_s[...].astype(BF16), cm_ref[...]) + d_ref[...] * u.astype(F32)
    gel = 0.5 * y * (1.0 + jnp.tanh(0.7978845608028654 * (y + 0.044715 * (y * y * y))))
    z = _dot(gel.astype(BF16), glu_ref[...])
    o_ref[...] = (z[:, 0:GROUP] * _sigmoid(z[:, GROUP:2 * GROUP])).astype(BF16)


def _s5_call(u_sb, bbar, ar, ai, cmat, dskip, glu_w, batch, seq):
    rows = u_sb.shape[0]
    steps = min(S5_STEPS, seq)
    tr = steps * batch
    nst = S5_GROUPS * S5_N
    const = lambda i: (0, 0)
    return pl.pallas_call(
        functools.partial(_s5_kernel, batch=batch),
        grid=(seq // steps,),
        in_specs=[pl.BlockSpec((tr, GROUP), lambda i: (i, 0)),
                  pl.BlockSpec((GROUP, 2 * nst), const),
                  pl.BlockSpec((1, nst), const),
                  pl.BlockSpec((1, nst), const),
                  pl.BlockSpec((2 * nst, GROUP), const),
                  pl.BlockSpec((1, GROUP), const),
                  pl.BlockSpec((GROUP, 2 * GROUP), const)],
        out_specs=pl.BlockSpec((tr, GROUP), lambda i: (i, 0)),
        out_shape=jax.ShapeDtypeStruct((rows, GROUP), BF16),
        scratch_shapes=[pltpu.VMEM((tr, 2 * nst), F32),
                        pltpu.VMEM((batch, nst), F32), pltpu.VMEM((batch, nst), F32)],
        compiler_params=_cparams(("arbitrary",)),
    )(u_sb, bbar, ar, ai, cmat, dskip, glu_w)


def _s5_params(a_re, a_im, log_step, b_re, b_im, c_re, c_im):
    lr = jnp.minimum(a_re.astype(F32), -1e-4)
    li = a_im.astype(F32)
    dt = jnp.exp(log_step.astype(F32))[:, None]
    mag = jnp.exp(lr * dt)
    ar = mag * jnp.cos(li * dt)
    ai = mag * jnp.sin(li * dt)
    den = lr * lr + li * li
    zr = ((ar - 1.0) * lr + ai * li) / den
    zi = (ai * lr - (ar - 1.0) * li) / den
    br = b_re.astype(F32)
    bi = b_im.astype(F32)
    bbr = zr[..., None] * br - zi[..., None] * bi
    bbi = zr[..., None] * bi + zi[..., None] * br
    eye = jnp.eye(S5_GROUPS, dtype=F32)
    def bd_in(m):
        return jnp.einsum('gnh,gk->ghkn', m, eye).reshape(GROUP, S5_GROUPS * S5_N)
    def bd_out(m):
        return jnp.einsum('ghn,gk->gnkh', m, eye).reshape(S5_GROUPS * S5_N, GROUP)
    bbar = jnp.concatenate([bd_in(bbr), bd_in(bbi)], axis=1).astype(BF16)
    cmat = jnp.concatenate([bd_out(c_re.astype(F32)), -bd_out(c_im.astype(F32))],
                           axis=0).astype(BF16)
    return bbar, ar.reshape(1, -1), ai.reshape(1, -1), cmat


def _outproj_kernel(yf_ref, yh_ref, yd_ref, ys_ref, wo_ref, x_ref, pg_ref, g1_ref, g2_ref,
                    sc_ref, sh_ref, rw_ref, rb_ref,
                    x1_ref, h2_ref, idx_ref, wt_ref):
    tm = x_ref.shape[0]
    y = (_dot(yf_ref[...], wo_ref[0:GROUP, :])
         + _dot(yh_ref[...], wo_ref[GROUP:2 * GROUP, :])
         + _dot(yd_ref[...], wo_ref[2 * GROUP:3 * GROUP, :])
         + _dot(ys_ref[...], wo_ref[3 * GROUP:4 * GROUP, :]))
    ms = jnp.mean(y * y, axis=-1, keepdims=True)
    x1 = x_ref[...] + g1_ref[0] * (y * lax.rsqrt(ms + EPS) * pg_ref[...])
    x1_ref[...] = x1
    ms2 = jnp.mean(x1 * x1, axis=-1, keepdims=True)
    h2 = (x1 * lax.rsqrt(ms2 + EPS) * g2_ref[...]) * (1.0 + sc_ref[0]) + sh_ref[0]
    h2_ref[...] = h2
    logits = jnp.dot(h2, rw_ref[...], preferred_element_type=F32,
                     precision=lax.Precision.HIGHEST) + rb_ref[...]
    lane = lax.broadcasted_iota(jnp.int32, (tm, LANES), 1)
    cur = logits
    vals, idxs = [], []
    for _ in range(TOP_K):
        m = jnp.max(cur, axis=-1, keepdims=True)
        sel = jnp.min(jnp.where(cur == m, lane, LANES), axis=-1, keepdims=True)
        vals.append(m)
        idxs.append(sel)
        cur = jnp.where(lane == sel, -jnp.inf, cur)
    es = [jnp.exp(v - vals[0]) for v in vals]
    tot = es[0] + es[1] + es[2] + es[3]
    idx_out = jnp.zeros((tm, LANES), jnp.int32)
    wt_out = jnp.zeros((tm, LANES), F32)
    for k in range(TOP_K):
        idx_out = jnp.where(lane == k, idxs[k], idx_out)
        wt_out = jnp.where(lane == k, es[k] / tot, wt_out)
    idx_ref[...] = idx_out
    wt_ref[...] = wt_out


def _outproj_call(yf, yh, yd, ys, w_out, x2, post_g, g1, pre_g2, sc2, sh2, rw, rb, batch, seq):
    t = x2.shape[0]
    tm = min(ROW_TILE, seq)
    nt = seq // tm
    row = lambda b, i: (b * nt + i, 0)
    const2 = lambda b, i: (0, 0)
    per_b = lambda b, i: (b, 0, 0)
    grp = pl.BlockSpec((tm, GROUP), row)
    vec = pl.BlockSpec((1, D_MODEL), const2)
    mod = pl.BlockSpec((1, 1, D_MODEL), per_b)
    return pl.pallas_call(
        _outproj_kernel,
        grid=(batch, nt),
        in_specs=[grp, grp, grp, grp,
                  pl.BlockSpec((D_MODEL, D_MODEL), const2),
                  pl.BlockSpec((tm, D_MODEL), row),
                  vec, mod, vec, mod, mod,
                  pl.BlockSpec((D_MODEL, LANES), const2),
                  pl.BlockSpec((1, LANES), const2)],
        out_specs=[pl.BlockSpec((tm, D_MODEL), row), pl.BlockSpec((tm, D_MODEL), row),
                   pl.BlockSpec((tm, LANES), row), pl.BlockSpec((tm, LANES), row)],
        out_shape=[jax.ShapeDtypeStruct((t, D_MODEL), F32), jax.ShapeDtypeStruct((t, D_MODEL), F32),
                   jax.ShapeDtypeStruct((t, LANES), jnp.int32),
                   jax.ShapeDtypeStruct((t, LANES), F32)],
        compiler_params=_cparams(("arbitrary", "arbitrary")),
    )(yf, yh, yd, ys, w_out, x2, post_g, g1, pre_g2, sc2, sh2, rw, rb)


def _row_copy(src_hbm, dst_vmem, src_row, dst_row, sem):
    return pltpu.make_async_copy(src_hbm.at[pl.ds(src_row, 1), :],
                                 dst_vmem.at[pl.ds(dst_row, 1), :], sem)


def _expert_kernel(te_ref, src_ref, h_hbm, w1_ref, b1_ref, w2_ref, b2_ref, o_ref, xbuf, sem):
    tm = xbuf.shape[0]

    def issue(j, _):
        _row_copy(h_hbm, xbuf, src_ref[0, 0, j], j, sem).start()
        return 0

    lax.fori_loop(0, tm, issue, 0)

    def wait(j, _):
        _row_copy(h_hbm, xbuf, 0, j, sem).wait()
        return 0

    lax.fori_loop(0, tm, wait, 0)
    xb = xbuf[...].astype(BF16)
    hh = _dot(xb, w1_ref[0]) + b1_ref[0]
    glu = jnp.minimum(hh[:, 0:D_FF], SWIGLU_LIMIT)
    lin = jnp.clip(hh[:, D_FF:2 * D_FF], -SWIGLU_LIMIT, SWIGLU_LIMIT)
    act = glu * _sigmoid(SWIGLU_ALPHA * glu) * (lin + 1.0)
    o_ref[...] = _dot(act.astype(BF16), w2_ref[0]) + b2_ref[0]


def _expert_call(tile_expert, src_rows, h2, w1, b1, w2, b2):
    ntiles = tile_expert.shape[0]
    tm = EXP_TILE
    grid_spec = pltpu.PrefetchScalarGridSpec(
        num_scalar_prefetch=1,
        grid=(ntiles,),
        in_specs=[pl.BlockSpec((1, 1, tm), lambda i, te: (i, 0, 0), memory_space=pltpu.SMEM),
                  pl.BlockSpec(memory_space=pl.ANY),
                  pl.BlockSpec((1, D_MODEL, 2 * D_FF), lambda i, te: (te[i], 0, 0)),
                  pl.BlockSpec((1, 1, 2 * D_FF), lambda i, te: (te[i], 0, 0)),
                  pl.BlockSpec((1, D_FF, D_MODEL), lambda i, te: (te[i], 0, 0)),
                  pl.BlockSpec((1, 1, D_MODEL), lambda i, te: (te[i], 0, 0))],
        out_specs=pl.BlockSpec((tm, D_MODEL), lambda i, te: (i, 0)),
        scratch_shapes=[pltpu.VMEM((tm, D_MODEL), F32), pltpu.SemaphoreType.DMA(())],
    )
    return pl.pallas_call(
        _expert_kernel,
        grid_spec=grid_spec,
        out_shape=jax.ShapeDtypeStruct((ntiles * tm, D_MODEL), F32),
        compiler_params=_cparams(("arbitrary",)),
    )(tile_expert, src_rows.reshape(ntiles, 1, tm), h2, w1, b1, w2, b2)


def _combine_kernel(pos_ref, y_hbm, wt_ref, x_ref, pg_ref, g_ref, o_ref, ybuf, sem):
    tm = x_ref.shape[0]

    def issue(j, _):
        for k in range(TOP_K):
            _row_copy(y_hbm, ybuf.at[k], pos_ref[0, 0, k * tm + j], j, sem).start()
        return 0

    lax.fori_loop(0, tm, issue, 0)

    def wait(j, _):
        for k in range(TOP_K):
            _row_copy(y_hbm, ybuf.at[k], 0, j, sem).wait()
        return 0

    lax.fori_loop(0, tm, wait, 0)
    wt = wt_ref[...]
    y = wt[:, 0:1] * ybuf[0]
    for k in range(1, TOP_K):
        y = y + wt[:, k:k + 1] * ybuf[k]
    ms = jnp.mean(y * y, axis=-1, keepdims=True)
    o_ref[...] = x_ref[...] + g_ref[0] * (y * lax.rsqrt(ms + EPS) * pg_ref[...])


def _combine_call(dest, y_sorted, wts, x1, post_g, g2, batch, seq):
    t = x1.shape[0]
    tm = min(CMB_TILE, seq)
    nt = seq // tm
    ntiles = t // tm
    dest_t = dest.reshape(ntiles, tm, TOP_K).transpose(0, 2, 1).reshape(ntiles, 1, TOP_K * tm)
    row = lambda b, i: (b * nt + i, 0)
    return pl.pallas_call(
        _combine_kernel,
        grid=(batch, nt),
        in_specs=[pl.BlockSpec((1, 1, TOP_K * tm), lambda b, i: (b * nt + i, 0, 0),
                               memory_space=pltpu.SMEM),
                  pl.BlockSpec(memory_space=pl.ANY),
                  pl.BlockSpec((tm, LANES), row),
                  pl.BlockSpec((tm, D_MODEL), row),
                  pl.BlockSpec((1, D_MODEL), lambda b, i: (0, 0)),
                  pl.BlockSpec((1, 1, D_MODEL), lambda b, i: (b, 0, 0))],
        out_specs=pl.BlockSpec((tm, D_MODEL), row),
        out_shape=jax.ShapeDtypeStruct((t, D_MODEL), F32),
        scratch_shapes=[pltpu.VMEM((TOP_K, tm, D_MODEL), F32), pltpu.SemaphoreType.DMA(())],
        compiler_params=_cparams(("arbitrary", "arbitrary")),
    )(dest_t, y_sorted, wts, x1, post_g, g2)


def _dispatch_plan(idx, t):
    tm = EXP_TILE
    npairs = t * TOP_K
    ntiles = npairs // tm + N_EXPERTS
    e_flat = idx.reshape(-1)
    pair = jnp.arange(npairs, dtype=jnp.int32)
    _, order = lax.sort((e_flat, pair), num_keys=1, is_stable=True)
    _, inv = lax.sort((order, pair), num_keys=1, is_stable=True)
    counts = jnp.sum(jax.nn.one_hot(e_flat, N_EXPERTS, dtype=jnp.int32), axis=0)
    cnt_excl = jnp.cumsum(counts) - counts
    tiles_per = (counts + tm - 1) // tm
    tile_end = jnp.cumsum(tiles_per)
    tile_start = tile_end - tiles_per
    tile_ids = jnp.arange(ntiles, dtype=jnp.int32)
    tile_expert = jnp.minimum(
        jnp.sum((tile_ids[:, None] >= tile_end[None, :]).astype(jnp.int32), axis=1),
        N_EXPERTS - 1).astype(jnp.int32)
    rows = jnp.arange(ntiles * tm, dtype=jnp.int32)
    row_e = jnp.repeat(tile_expert, tm)
    off = rows - tile_start[row_e] * tm
    valid = (off < counts[row_e]) & (rows < tile_end[-1] * tm)
    sidx = jnp.clip(cnt_excl[row_e] + off, 0, npairs - 1)
    src_rows = jnp.where(valid, order[sidx] // TOP_K, 0).astype(jnp.int32)
    dest = (tile_start[e_flat] * tm + inv - cnt_excl[e_flat]).astype(jnp.int32)
    return tile_expert, src_rows, dest.reshape(t, TOP_K)


def _rope_tables():
    lane = np.arange(GROUP)
    d = lane % DIFF_D
    rd = DIFF_D // 4
    half = rd // 2
    inv = np.where(d < rd, ROPE_THETA ** (-(d % half).astype(np.float32) / half), 0.0)
    ma = np.where(d < half, -1.0, 0.0)
    mb = np.where((d >= half) & (d < rd), 1.0, 0.0)
    f = lambda a: jnp.asarray(a.reshape(1, GROUP), F32)
    return f(inv), f(ma), f(mb)


def kernel(x, c, positions, ada_w, ada_b, pre_norm_g, post_norm_g, w_in, w_out, fox_fb, hg_lower, hg_norm_g, diff_lam_q1, diff_lam_k1, diff_lam_q2, diff_lam_k2, diff_subln_g, s5_a_re, s5_a_im, s5_log_step, s5_b_re, s5_b_im, s5_c_re, s5_c_im, s5_d, s5_glu_w, router_w, router_b, exp_w1, exp_b1, exp_w2, exp_b2):
    batch, seq, d = x.shape
    depth = ada_w.shape[0]
    t = batch * seq
    assert d == D_MODEL and seq % ATT_TQ == 0

    lb_all = jnp.cumsum(jax.nn.softmax(hg_lower.astype(F32), axis=0), axis=0)
    lb_all = lb_all - lb_all[0:1]
    mod = _ada_call(c, ada_w, ada_b)
    inv_lane, mask_a, mask_b = _rope_tables()
    pos = positions.reshape(t, 1).astype(jnp.int32)
    xf = x.reshape(t, d)

    fcols = 3 * GROUP
    for l in range(depth):
        m6 = mod[l].reshape(batch, 6, 1, d)
        sh1, sc1, g1, sh2, sc2, g2 = [m6[:, j] for j in range(6)]
        wl = w_in[l]
        w_main = jnp.concatenate([wl[:, 0:fcols], wl[:, fcols + FOX_HEADS:]], axis=1).astype(BF16)
        w_f = jnp.pad(wl[:, fcols:fcols + FOX_HEADS], ((0, 0), (0, LANES - FOX_HEADS))).astype(BF16)
        fb = jnp.pad(fox_fb[l], (0, LANES - FOX_HEADS)).reshape(1, LANES)
        fox, fcol, frow, hg, diff, s5u = _inproj_call(
            xf, pre_norm_g[l, 0].reshape(1, d), sc1, sh1, w_main, w_f, fb, pos,
            inv_lane, mask_a, mask_b, batch, seq)

        y_fox = _fox_call(fox, fcol, frow, batch, seq)

        lam_init = 0.8 - 0.6 * math.exp(-0.3 * l)
        lam = (jnp.exp(jnp.sum(diff_lam_q1[l].astype(F32) * diff_lam_k1[l].astype(F32)))
               - jnp.exp(jnp.sum(diff_lam_q2[l].astype(F32) * diff_lam_k2[l].astype(F32))) + lam_init)
        y_diff = _diff_call(diff, lam.reshape(1), diff_subln_g[l].reshape(1, DIFF_DV),
                            1.0 - lam_init, batch, seq)

        y_hg = _hgrn2_call(hg, lb_all[l].reshape(1, -1), hg_norm_g[l].reshape(1, HG_DV), batch, seq)

        bbar, ar, ai, cmat = _s5_params(s5_a_re[l], s5_a_im[l], s5_log_step[l], s5_b_re[l],
                                        s5_b_im[l], s5_c_re[l], s5_c_im[l])
        u_sb = s5u.reshape(batch, seq, GROUP).transpose(1, 0, 2).reshape(t, GROUP)
        y_s5_sb = _s5_call(u_sb, bbar, ar, ai, cmat, s5_d[l].reshape(1, GROUP),
                           s5_glu_w[l].astype(BF16), batch, seq)
        y_s5 = y_s5_sb.reshape(seq, batch, GROUP).transpose(1, 0, 2).reshape(t, GROUP)

        rw = jnp.pad(router_w[l], ((0, 0), (0, LANES - N_EXPERTS)))
        rb = jnp.pad(router_b[l], (0, LANES - N_EXPERTS), constant_values=NEG_BIG).reshape(1, LANES)
        x1, h2, idx, wts = _outproj_call(
            y_fox, y_hg, y_diff, y_s5, w_out[l].astype(BF16), xf,
            post_norm_g[l, 0].reshape(1, d), g1, pre_norm_g[l, 1].reshape(1, d), sc2, sh2,
            rw, rb, batch, seq)

        tile_expert, src_rows, dest = _dispatch_plan(idx[:, 0:TOP_K], t)
        w1 = exp_w1[l]
        w1p = jnp.concatenate([w1[:, :, 0::2], w1[:, :, 1::2]], axis=-1).astype(BF16)
        b1 = exp_b1[l]
        b1p = jnp.concatenate([b1[:, 0::2], b1[:, 1::2]], axis=-1).reshape(N_EXPERTS, 1, 2 * D_FF)
        y_sorted = _expert_call(tile_expert, src_rows, h2, w1p, b1p, exp_w2[l].astype(BF16),
                                exp_b2[l].reshape(N_EXPERTS, 1, D_MODEL))
        xf = _combine_call(dest, y_sorted, wts, x1, post_norm_g[l, 1].reshape(1, d), g2, batch, seq)

    return xf.reshape(batch, seq, d)
```

```python
import functools
import math

import numpy as np
import jax
import jax.numpy as jnp
from jax import lax
from jax.experimental import pallas as pl
from jax.experimental.pallas import tpu as pltpu

F32 = jnp.float32
BF16 = jnp.bfloat16

D_MODEL = 1024
GROUP = 256
FOX_HEADS = 4
FOX_HD = 64
HG_HEADS = 4
HG_DK = 128
HG_DV = 64
DIFF_HEADS = 4
DIFF_DV = 64
DIFF_D = 32
S5_CH = 16
S5_GROUPS = 16
S5_N = 64
ROPE_THETA = 500000.0
N_EXPERTS = 32
TOP_K = 4
D_FF = 1024
SWIGLU_LIMIT = 7.0
SWIGLU_ALPHA = 1.702
EPS = 1e-6
NEG_BIG = -1e30

LANES = 128
VMEM_LIMIT = 56 * 1024 * 1024

LOG2E = 1.4426950408889634

ROW_TILE = 512
ATT_T = ROW_TILE
HG_CHUNK = 64
S5_STEPS = 128
EXP_TILE = 256
CMB_TILE = 256

HG_COLS = 2 * HG_HEADS * HG_DK + 2 * GROUP
MAIN_COLS = 3 * GROUP + HG_COLS + 3 * GROUP + GROUP


def _cparams(sem):
    return pltpu.CompilerParams(dimension_semantics=sem, vmem_limit_bytes=VMEM_LIMIT)


def _sigmoid(x):
    return 1.0 / (1.0 + jnp.exp(-x))


def _shr(x, pow2):
    return lax.shift_right_logical(x, int(math.log2(pow2)))


def _dot(a, b):
    return jnp.dot(a, b, preferred_element_type=F32)


def _dot_nt(a, b):
    return lax.dot_general(a, b, (((1,), (1,)), ((), ())), preferred_element_type=F32)


def _dot_tn(a, b):
    return lax.dot_general(a, b, (((0,), (0,)), ((), ())), preferred_element_type=F32)


def _split_dot(m_bf16, x):
    hi = x.astype(BF16)
    r1 = x - hi.astype(F32)
    mid = r1.astype(BF16)
    lo = (r1 - mid.astype(F32)).astype(BF16)
    return _dot(m_bf16, hi) + _dot(m_bf16, mid) + _dot(m_bf16, lo)


def _ada_kernel(c_ref, w_ref, b_ref, o_ref):
    c = c_ref[...]
    cs = c * _sigmoid(c)
    o_ref[0] = jnp.dot(cs, w_ref[0], preferred_element_type=F32,
                       precision=lax.Precision.HIGHEST) + b_ref[0]


def _ada_call(c, ada_w, ada_b):
    depth, d, n = ada_w.shape
    b = c.shape[0]
    tn = 1536
    return pl.pallas_call(
        _ada_kernel,
        grid=(depth, n // tn),
        in_specs=[pl.BlockSpec((b, d), lambda l, j: (0, 0)),
                  pl.BlockSpec((1, d, tn), lambda l, j: (l, 0, j)),
                  pl.BlockSpec((1, 1, tn), lambda l, j: (l, 0, j))],
        out_specs=pl.BlockSpec((1, b, tn), lambda l, j: (l, 0, j)),
        out_shape=jax.ShapeDtypeStruct((depth, b, n), F32),
        compiler_params=_cparams(("arbitrary", "arbitrary")),
    )(c, ada_w, ada_b.reshape(depth, 1, n))


def _inproj_kernel(x_ref, g_ref, sc_ref, sh_ref, w_ref, wf_ref, fb_ref, pos_ref, inv_ref,
                   ma_ref, mb_ref,
                   fox_ref, frow_ref, hg_ref, diff_ref, s5_ref, carry_ref):
    i = pl.program_id(1)
    tm = x_ref.shape[0]

    @pl.when(i == 0)
    def _():
        carry_ref[...] = jnp.zeros_like(carry_ref)

    x = x_ref[...]
    ms = jnp.mean(x * x, axis=-1, keepdims=True)
    h = (x * lax.rsqrt(ms + EPS) * g_ref[...]) * (1.0 + sc_ref[0]) + sh_ref[0]
    hb = h.astype(BF16)

    o = 0
    pf = _dot(hb, w_ref[:, o:o + 3 * GROUP])
    fox_ref[:, 0:GROUP] = (pf[:, 0:GROUP] * (FOX_HD ** -0.5 * LOG2E)).astype(BF16)
    fox_ref[:, GROUP:3 * GROUP] = pf[:, GROUP:3 * GROUP].astype(BF16)
    o += 3 * GROUP

    ff = _dot(hb, wf_ref[...]) + fb_ref[...]
    logf = jnp.minimum(ff, 0.0) - jnp.log(1.0 + jnp.exp(-jnp.abs(ff)))
    r = lax.broadcasted_iota(jnp.int32, (tm, tm), 0)
    cidx = lax.broadcasted_iota(jnp.int32, (tm, tm), 1)
    tri = jnp.where(cidx <= r, 1.0, 0.0).astype(BF16)
    cum = _split_dot(tri, logf) + carry_ref[...]
    carry_ref[...] = cum[tm - 1:tm, :]
    cum_t = jnp.transpose(cum * LOG2E)
    frow_ref[0, 0] = cum_t[0:8, :]

    hg_ref[...] = _dot(hb, w_ref[:, o:o + HG_COLS]).astype(BF16)
    o += HG_COLS

    pd = _dot(hb, w_ref[:, o:o + 3 * GROUP])
    o += 3 * GROUP
    pos = pos_ref[...].astype(F32)
    ang = pos * inv_ref[...]
    cos = jnp.cos(ang)
    sin = jnp.sin(ang)
    ma = ma_ref[...]
    mb = mb_ref[...]

    def rope(xx):
        outs = []
        for j in range(GROUP // LANES):
            sl = slice(j * LANES, (j + 1) * LANES)
            xs = xx[:, sl]
            partner = (pltpu.roll(xs, LANES - 4, axis=1) * ma[:, sl]
                       + pltpu.roll(xs, 4, axis=1) * mb[:, sl])
            outs.append(xs * cos[:, sl] + partner * sin[:, sl])
        return jnp.concatenate(outs, axis=1)

    diff_ref[:, 0:GROUP] = (rope(pd[:, 0:GROUP]) * (DIFF_D ** -0.5 * LOG2E)).astype(BF16)
    diff_ref[:, GROUP:2 * GROUP] = rope(pd[:, GROUP:2 * GROUP]).astype(BF16)
    diff_ref[:, 2 * GROUP:3 * GROUP] = pd[:, 2 * GROUP:3 * GROUP].astype(BF16)

    s5_ref[...] = _dot(hb, w_ref[:, o:o + GROUP]).astype(BF16)


def _inproj_call(x2, g_pre, sc, sh, w_main, w_f, fb, pos, inv_lane, mask_a, mask_b, batch, seq):
    t = x2.shape[0]
    tm = ROW_TILE
    nt = seq // tm
    row = lambda b, i: (b * nt + i, 0)
    const2 = lambda b, i: (0, 0)
    per_b = lambda b, i: (b, 0, 0)
    outs = pl.pallas_call(
        _inproj_kernel,
        grid=(batch, nt),
        in_specs=[pl.BlockSpec((tm, D_MODEL), row),
                  pl.BlockSpec((1, D_MODEL), const2),
                  pl.BlockSpec((1, 1, D_MODEL), per_b),
                  pl.BlockSpec((1, 1, D_MODEL), per_b),
                  pl.BlockSpec((D_MODEL, MAIN_COLS), const2),
                  pl.BlockSpec((D_MODEL, LANES), const2),
                  pl.BlockSpec((1, LANES), const2),
                  pl.BlockSpec((tm, 1), row),
                  pl.BlockSpec((1, GROUP), const2),
                  pl.BlockSpec((1, GROUP), const2),
                  pl.BlockSpec((1, GROUP), const2)],
        out_specs=[pl.BlockSpec((tm, 3 * GROUP), row),
                   pl.BlockSpec((1, 1, 8, tm), lambda b, i: (b, i, 0, 0)),
                   pl.BlockSpec((tm, HG_COLS), row),
                   pl.BlockSpec((tm, 3 * GROUP), row),
                   pl.BlockSpec((tm, GROUP), row)],
        out_shape=[jax.ShapeDtypeStruct((t, 3 * GROUP), BF16),
                   jax.ShapeDtypeStruct((batch, nt, 8, tm), F32),
                   jax.ShapeDtypeStruct((t, HG_COLS), BF16),
                   jax.ShapeDtypeStruct((t, 3 * GROUP), BF16),
                   jax.ShapeDtypeStruct((t, GROUP), BF16)],
        scratch_shapes=[pltpu.VMEM((1, LANES), F32)],
        compiler_params=_cparams(("arbitrary", "arbitrary")),
    )(x2, g_pre, sc, sh, w_main, w_f, fb, pos, inv_lane, mask_a, mask_b)
    return outs


_HEAD_W = 64
_HEADS = 4


def _slot(x64):
    return jnp.concatenate([x64, jnp.zeros_like(x64)], axis=1)


def _pad_kv(k_ref, v_ref, k_s, v_s):
    seq = k_ref.shape[0]
    rows = ATT_T
    lane = lax.broadcasted_iota(jnp.int32, (rows, LANES), 1)

    def body(c, _):
        r0 = pl.multiple_of(c * rows, rows)
        for h in range(_HEADS):
            k_s[pl.ds(r0, rows), h * LANES:(h + 1) * LANES] = _slot(
                k_ref[pl.ds(r0, rows), h * _HEAD_W:(h + 1) * _HEAD_W])
            v = _slot(v_ref[pl.ds(r0, rows), h * _HEAD_W:(h + 1) * _HEAD_W])
            v_s[pl.ds(r0, rows), h * LANES:(h + 1) * LANES] = jnp.where(
                lane == _HEAD_W, jnp.ones_like(v), v)
        return 0

    lax.fori_loop(0, seq // rows, body, 0)


def _attend(qs, slots, k_s, v_s, qi, key_bias):
    n = len(qs)
    tq = qs[0].shape[0]
    tk = ATT_T

    def block(kb, carry, masked):
        ms, accs = carry
        start = pl.multiple_of(kb * tk, tk)
        if masked:
            rr = lax.broadcasted_iota(jnp.int32, (tq, tk), 0)
            cc = lax.broadcasted_iota(jnp.int32, (tq, tk), 1)
            keep = cc <= rr
        new_m, new_acc = [], []
        for j in range(n):
            c0 = slots[j] * LANES
            s = _dot_nt(qs[j], k_s[pl.ds(start, tk), c0:c0 + LANES])
            kbias = key_bias(j, kb)
            if kbias is not None:
                s = s - kbias
            if masked:
                s = jnp.where(keep, s, NEG_BIG)
            m_new = jnp.maximum(ms[j], jnp.max(s, axis=-1, keepdims=True))
            alpha = jnp.exp2(ms[j] - m_new)
            p = jnp.exp2(s - m_new).astype(BF16)
            new_acc.append(alpha * accs[j] + _dot(p, v_s[pl.ds(start, tk), c0:c0 + LANES]))
            new_m.append(m_new)
        return tuple(new_m), tuple(new_acc)

    init = (tuple(jnp.full((tq, 1), NEG_BIG, F32) for _ in range(n)),
            tuple(jnp.zeros((tq, LANES), F32) for _ in range(n)))
    carry = lax.fori_loop(0, qi, lambda kb, c: block(kb, c, False), init)
    _, accs = block(qi, carry, True)
    return accs


def _fox_kernel(q_ref, k_ref, v_ref, frow_ref, o_ref, k_s, v_s):
    qi = pl.program_id(1)

    @pl.when(qi == 0)
    def _():
        _pad_kv(k_ref, v_ref, k_s, v_s)

    qs = [_slot(q_ref[:, h * _HEAD_W:(h + 1) * _HEAD_W]) for h in range(_HEADS)]
    accs = _attend(qs, list(range(_HEADS)), k_s, v_s, qi,
                   lambda j, kb: frow_ref[0, kb, j:j + 1, :])
    outs = [a[:, 0:_HEAD_W] / a[:, _HEAD_W:_HEAD_W + 1] for a in accs]
    o_ref[...] = jnp.concatenate(outs, axis=1).astype(BF16)


def _att_specs(seq):
    nq = seq // ATT_T
    return ([pl.BlockSpec((ATT_T, GROUP), lambda b, i: (b * nq + i, 0)),
             pl.BlockSpec((seq, GROUP), lambda b, i: (b, 1)),
             pl.BlockSpec((seq, GROUP), lambda b, i: (b, 2))],
            pl.BlockSpec((ATT_T, GROUP), lambda b, i: (b * nq + i, 0)),
            [pltpu.VMEM((seq, _HEADS * LANES), BF16), pltpu.VMEM((seq, _HEADS * LANES), BF16)])


def _fox_call(fox, frow, batch, seq):
    t = fox.shape[0]
    nq = seq // ATT_T
    qkv_specs, out_spec, scratch = _att_specs(seq)
    return pl.pallas_call(
        _fox_kernel,
        grid=(batch, nq),
        in_specs=qkv_specs + [pl.BlockSpec((1, nq, 8, ATT_T), lambda b, i: (b, 0, 0, 0))],
        out_specs=out_spec,
        out_shape=jax.ShapeDtypeStruct((t, GROUP), BF16),
        scratch_shapes=scratch,
        compiler_params=_cparams(("arbitrary", "arbitrary")),
    )(fox, fox, fox, frow)


def _diff_kernel(lam_ref, q_ref, k_ref, v_ref, g_ref, o_ref, k_s, v_s, *, out_scale):
    qi = pl.program_id(1)

    @pl.when(qi == 0)
    def _():
        _pad_kv(k_ref, v_ref, k_s, v_s)

    lam = lam_ref[0]
    lane = lax.broadcasted_iota(jnp.int32, (ATT_T, LANES), 1)
    outs = []
    for pair in range(_HEADS // 2):
        qs, slots = [], []
        for h in (2 * pair, 2 * pair + 1):
            q = _slot(q_ref[:, h * _HEAD_W:(h + 1) * _HEAD_W])
            qs += [jnp.where(lane < DIFF_D, q, jnp.zeros_like(q)),
                   jnp.where(lane >= DIFF_D, q, jnp.zeros_like(q))]
            slots += [h, h]
        accs = _attend(qs, slots, k_s, v_s, qi, lambda j, kb: None)
        for j in (0, 2):
            y = (accs[j][:, 0:_HEAD_W] / accs[j][:, _HEAD_W:_HEAD_W + 1]
                 - lam * (accs[j + 1][:, 0:_HEAD_W] / accs[j + 1][:, _HEAD_W:_HEAD_W + 1]))
            ms = jnp.mean(y * y, axis=-1, keepdims=True)
            outs.append(y * lax.rsqrt(ms + EPS) * g_ref[...] * out_scale)
    o_ref[...] = jnp.concatenate(outs, axis=1).astype(BF16)


def _diff_call(diff, lam, subln_g, out_scale, batch, seq):
    t = diff.shape[0]
    nq = seq // ATT_T
    qkv_specs, out_spec, scratch = _att_specs(seq)
    return pl.pallas_call(
        functools.partial(_diff_kernel, out_scale=out_scale),
        grid=(batch, nq),
        in_specs=([pl.BlockSpec(memory_space=pltpu.SMEM)] + qkv_specs
                  + [pl.BlockSpec((1, DIFF_DV), lambda b, i: (0, 0))]),
        out_specs=out_spec,
        out_shape=jax.ShapeDtypeStruct((t, GROUP), BF16),
        scratch_shapes=scratch,
        compiler_params=_cparams(("arbitrary", "arbitrary")),
    )(lam, diff, diff, diff, subln_g)


_HG_LEVELS = (32, 16, 8)


def _hg_ref_rows(G, level):
    C = G.shape[0]
    rows = []
    for tile in range(C // 8):
        if level is None:
            r = tile * 8 + 3
        else:
            grp = (tile * 8) // (2 * level)
            r = grp * 2 * level + level - 1
        rows.append(jnp.broadcast_to(G[r:r + 1, :], (8, G.shape[1])))
    return jnp.concatenate(rows, axis=0)


def _hgrn2_kernel(p_ref, lb_ref, ng_ref, o_ref, q_s, g_s, k_s, state_ref):
    i = pl.program_id(1)
    tm = p_ref.shape[0]
    C = HG_CHUNK
    KW = HG_HEADS * HG_DK

    @pl.when(i == 0)
    def _():
        state_ref[...] = jnp.zeros_like(state_ref)

    hq = p_ref[:, 0:KW].astype(F32)
    q_s[...] = hq * _sigmoid(hq)
    lb = lb_ref[...]
    f = lb + (1.0 - lb) * _sigmoid(p_ref[:, KW:2 * KW].astype(F32))
    k_s[...] = 1.0 - f
    r = lax.broadcasted_iota(jnp.int32, (tm, tm), 0)
    c = lax.broadcasted_iota(jnp.int32, (tm, tm), 1)
    bd = jnp.where((c <= r) & (_shr(r, C) == _shr(c, C)), 1.0, 0.0).astype(BF16)
    g_s[...] = _split_dot(bd, jnp.log(f))

    rr = lax.broadcasted_iota(jnp.int32, (C, C), 0)
    cc = lax.broadcasted_iota(jnp.int32, (C, C), 1)
    level_masks = []
    for m in _HG_LEVELS:
        same_grp = _shr(rr, 2 * m) == _shr(cc, 2 * m)
        level_masks.append(same_grp & ((rr & (2 * m - 1)) >= m) & ((cc & (2 * m - 1)) < m))
    diag_mask = (_shr(rr, 8) == _shr(cc, 8)) & (cc <= rr)
    row_idx = lax.broadcasted_iota(jnp.int32, (C, HG_DK), 0)

    def chunk(ci, _):
        base = pl.multiple_of(ci * C, C)
        for h in range(HG_HEADS):
            ks = slice(h * HG_DK, (h + 1) * HG_DK)
            q = q_s[pl.ds(base, C), ks]
            k = k_s[pl.ds(base, C), ks]
            G = g_s[pl.ds(base, C), ks]
            v = p_ref[pl.ds(base, C), 2 * KW + h * HG_DV:2 * KW + (h + 1) * HG_DV]
            st = state_ref[h]
            gl = G[C - 1:C, :]
            o = _dot_nt((q * jnp.exp(G)).astype(BF16), st.astype(BF16))
            a = jnp.zeros((C, C), F32)
            for m, mask in zip(_HG_LEVELS, level_masks):
                ref = _hg_ref_rows(G, m)
                upper = (row_idx & (2 * m - 1)) >= m
                e = jnp.exp(jnp.where(upper, G - ref, ref - G))
                am = _dot_nt((q * e).astype(BF16), (k * e).astype(BF16))
                a = a + jnp.where(mask, am, 0.0)
            ref = _hg_ref_rows(G, None)
            d = jnp.clip(G - ref, -80.0, 80.0)
            am = _dot_nt((q * jnp.exp(d)).astype(BF16), (k * jnp.exp(-d)).astype(BF16))
            a = a + jnp.where(diag_mask, am, 0.0)
            o = o + _dot(a.astype(BF16), v)
            kd = (k * jnp.exp(gl - G)).astype(BF16)
            state_ref[h] = jnp.exp(gl) * st + _dot_tn(v, kd)
            ms = jnp.mean(o * o, axis=-1, keepdims=True)
            on = o * lax.rsqrt(ms + EPS) * ng_ref[...]
            gate = p_ref[pl.ds(base, C), 2 * KW + GROUP + h * HG_DV:
                         2 * KW + GROUP + (h + 1) * HG_DV].astype(F32)
            o_ref[pl.ds(base, C), h * HG_DV:(h + 1) * HG_DV] = (
                on * (gate * _sigmoid(gate))).astype(BF16)
        return 0

    lax.fori_loop(0, tm // C, chunk, 0)


def _hgrn2_call(hg, lb, norm_g, batch, seq):
    t = hg.shape[0]
    tm = min(ROW_TILE, seq)
    nt = seq // tm
    kw = HG_HEADS * HG_DK
    return pl.pallas_call(
        _hgrn2_kernel,
        grid=(batch, nt),
        in_specs=[pl.BlockSpec((tm, HG_COLS), lambda b, i: (b * nt + i, 0)),
                  pl.BlockSpec((1, kw), lambda b, i: (0, 0)),
                  pl.BlockSpec((1, HG_DV), lambda b, i: (0, 0))],
        out_specs=pl.BlockSpec((tm, GROUP), lambda b, i: (b * nt + i, 0)),
        out_shape=jax.ShapeDtypeStruct((t, GROUP), BF16),
        scratch_shapes=[pltpu.VMEM((tm, kw), F32), pltpu.VMEM((tm, kw), F32),
                        pltpu.VMEM((tm, kw), F32),
                        pltpu.VMEM((HG_HEADS, HG_DV, HG_DK), F32)],
        compiler_params=_cparams(("arbitrary", "arbitrary")),
    )(hg, lb, norm_g)


def _s5_kernel(u_ref, bb_ref, ar_ref, ai_ref, cm_ref, d_ref, glu_ref, o_ref, bu_s, xr_s, xi_s,
               *, batch):
    i = pl.program_id(0)
    nst = S5_GROUPS * S5_N
    steps = u_ref.shape[0] // batch

    @pl.when(i == 0)
    def _():
        xr_s[...] = jnp.zeros_like(xr_s)
        xi_s[...] = jnp.zeros_like(xi_s)

    u = u_ref[...]
    bu_s[...] = _dot(u, bb_ref[...])
    ar = jnp.broadcast_to(ar_ref[...], (batch, nst))
    ai = jnp.broadcast_to(ai_ref[...], (batch, nst))

    def step(t, carry):
        xr, xi = carry
        r0 = pl.multiple_of(t * batch, batch)
        nr = ar * xr - ai * xi + bu_s[pl.ds(r0, batch), 0:nst]
        ni = ar * xi + ai * xr + bu_s[pl.ds(r0, batch), nst:2 * nst]
        bu_s[pl.ds(r0, batch), 0:nst] = nr
        bu_s[pl.ds(r0, batch), nst:2 * nst] = ni
        return nr, ni

    xr, xi = lax.fori_loop(0, steps, step, (xr_s[...], xi_s[...]))
    xr_s[...] = xr
    xi_s[...] = xi
    y = _dot(bu_s[...].astype(BF16), cm_ref[...]) + d_ref[...] * u.astype(F32)
    gel = 0.5 * y * (1.0 + jnp.tanh(0.7978845608028654 * (y + 0.044715 * (y * y * y))))
    z = _dot(gel.astype(BF16), glu_ref[...])
    o_ref[...] = (z[:, 0:GROUP] * _sigmoid(z[:, GROUP:2 * GROUP])).astype(BF16)


def _s5_call(u_sb, bbar, ar, ai, cmat, dskip, glu_w, batch, seq):
    rows = u_sb.shape[0]
    steps = min(S5_STEPS, seq)
    tr = steps * batch
    nst = S5_GROUPS * S5_N
    const = lambda i: (0, 0)
    return pl.pallas_call(
        functools.partial(_s5_kernel, batch=batch),
        grid=(seq // steps,),
        in_specs=[pl.BlockSpec((tr, GROUP), lambda i: (i, 0)),
                  pl.BlockSpec((GROUP, 2 * nst), const),
                  pl.BlockSpec((1, nst), const),
                  pl.BlockSpec((1, nst), const),
                  pl.BlockSpec((2 * nst, GROUP), const),
                  pl.BlockSpec((1, GROUP), const),
                  pl.BlockSpec((GROUP, 2 * GROUP), const)],
        out_specs=pl.BlockSpec((tr, GROUP), lambda i: (i, 0)),
        out_shape=jax.ShapeDtypeStruct((rows, GROUP), BF16),
        scratch_shapes=[pltpu.VMEM((tr, 2 * nst), F32),
                        pltpu.VMEM((batch, nst), F32), pltpu.VMEM((batch, nst), F32)],
        compiler_params=_cparams(("arbitrary",)),
    )(u_sb, bbar, ar, ai, cmat, dskip, glu_w)


def _s5_params(a_re, a_im, log_step, b_re, b_im, c_re, c_im):
    lr = jnp.minimum(a_re.astype(F32), -1e-4)
    li = a_im.astype(F32)
    dt = jnp.exp(log_step.astype(F32))[:, None]
    mag = jnp.exp(lr * dt)
    ar = mag * jnp.cos(li * dt)
    ai = mag * jnp.sin(li * dt)
    den = lr * lr + li * li
    zr = ((ar - 1.0) * lr + ai * li) / den
    zi = (ai * lr - (ar - 1.0) * li) / den
    br = b_re.astype(F32)
    bi = b_im.astype(F32)
    bbr = zr[..., None] * br - zi[..., None] * bi
    bbi = zr[..., None] * bi + zi[..., None] * br
    eye = jnp.eye(S5_GROUPS, dtype=F32)
    def bd_in(m):
        return jnp.einsum('gnh,gk->ghkn', m, eye).reshape(GROUP, S5_GROUPS * S5_N)
    def bd_out(m):
        return jnp.einsum('ghn,gk->gnkh', m, eye).reshape(S5_GROUPS * S5_N, GROUP)
    bbar = jnp.concatenate([bd_in(bbr), bd_in(bbi)], axis=1).astype(BF16)
    cmat = jnp.concatenate([bd_out(c_re.astype(F32)), -bd_out(c_im.astype(F32))],
                           axis=0).astype(BF16)
    return bbar, ar.reshape(1, -1), ai.reshape(1, -1), cmat


def _outproj_kernel(yf_ref, yh_ref, yd_ref, ys_ref, wo_ref, x_ref, pg_ref, g1_ref, g2_ref,
                    sc_ref, sh_ref, rw_ref, rb_ref,
                    x1_ref, h2_ref, idx_ref, wt_ref):
    tm = x_ref.shape[0]
    y = (_dot(yf_ref[...], wo_ref[0:GROUP, :])
         + _dot(yh_ref[...], wo_ref[GROUP:2 * GROUP, :])
         + _dot(yd_ref[...], wo_ref[2 * GROUP:3 * GROUP, :])
         + _dot(ys_ref[...], wo_ref[3 * GROUP:4 * GROUP, :]))
    ms = jnp.mean(y * y, axis=-1, keepdims=True)
    x1 = x_ref[...] + g1_ref[0] * (y * lax.rsqrt(ms + EPS) * pg_ref[...])
    x1_ref[...] = x1
    ms2 = jnp.mean(x1 * x1, axis=-1, keepdims=True)
    h2 = (x1 * lax.rsqrt(ms2 + EPS) * g2_ref[...]) * (1.0 + sc_ref[0]) + sh_ref[0]
    h2_ref[...] = h2
    logits = jnp.dot(h2, rw_ref[...], preferred_element_type=F32,
                     precision=lax.Precision.HIGHEST) + rb_ref[...]
    lane = lax.broadcasted_iota(jnp.int32, (tm, LANES), 1)
    cur = logits
    vals, idxs = [], []
    for _ in range(TOP_K):
        m = jnp.max(cur, axis=-1, keepdims=True)
        sel = jnp.min(jnp.where(cur == m, lane, LANES), axis=-1, keepdims=True)
        vals.append(m)
        idxs.append(sel)
        cur = jnp.where(lane == sel, -jnp.inf, cur)
    es = [jnp.exp(v - vals[0]) for v in vals]
    tot = es[0] + es[1] + es[2] + es[3]
    idx_out = jnp.zeros((tm, LANES), jnp.int32)
    wt_out = jnp.zeros((tm, LANES), F32)
    for k in range(TOP_K):
        idx_out = jnp.where(lane == k, idxs[k], idx_out)
        wt_out = jnp.where(lane == k, es[k] / tot, wt_out)
    idx_ref[...] = idx_out
    wt_ref[...] = wt_out


def _outproj_call(yf, yh, yd, ys, w_out, x2, post_g, g1, pre_g2, sc2, sh2, rw, rb, batch, seq):
    t = x2.shape[0]
    tm = min(ROW_TILE, seq)
    nt = seq // tm
    row = lambda b, i: (b * nt + i, 0)
    const2 = lambda b, i: (0, 0)
    per_b = lambda b, i: (b, 0, 0)
    grp = pl.BlockSpec((tm, GROUP), row)
    vec = pl.BlockSpec((1, D_MODEL), const2)
    mod = pl.BlockSpec((1, 1, D_MODEL), per_b)
    return pl.pallas_call(
        _outproj_kernel,
        grid=(batch, nt),
        in_specs=[grp, grp, grp, grp,
                  pl.BlockSpec((D_MODEL, D_MODEL), const2),
                  pl.BlockSpec((tm, D_MODEL), row),
                  vec, mod, vec, mod, mod,
                  pl.BlockSpec((D_MODEL, LANES), const2),
                  pl.BlockSpec((1, LANES), const2)],
        out_specs=[pl.BlockSpec((tm, D_MODEL), row), pl.BlockSpec((tm, D_MODEL), row),
                   pl.BlockSpec((tm, LANES), row), pl.BlockSpec((tm, LANES), row)],
        out_shape=[jax.ShapeDtypeStruct((t, D_MODEL), F32), jax.ShapeDtypeStruct((t, D_MODEL), F32),
                   jax.ShapeDtypeStruct((t, LANES), jnp.int32),
                   jax.ShapeDtypeStruct((t, LANES), F32)],
        compiler_params=_cparams(("arbitrary", "arbitrary")),
    )(yf, yh, yd, ys, w_out, x2, post_g, g1, pre_g2, sc2, sh2, rw, rb)


def _row_copy(src_hbm, dst_vmem, src_row, dst_row, sem):
    return pltpu.make_async_copy(src_hbm.at[pl.ds(src_row, 1), :],
                                 dst_vmem.at[pl.ds(dst_row, 1), :], sem)


def _start_rows(idx_ref, n, src_hbm, dst_vmem, sem, idx_off=0):
    for j in range(n):
        _row_copy(src_hbm, dst_vmem, idx_ref[0, 0, idx_off + j], j, sem).start()


def _wait_rows(n, src_hbm, dst_vmem, sem):
    for j in range(n):
        _row_copy(src_hbm, dst_vmem, 0, j, sem).wait()


def _w1_prep_kernel(w_ref, p_ref, o_ref):
    chunk = 2 * LANES
    for c in range(2 * D_FF // chunk):
        r = _dot(w_ref[0, :, c * chunk:(c + 1) * chunk].astype(BF16), p_ref[...])
        o_ref[0, :, c * LANES:(c + 1) * LANES] = r[:, 0:LANES].astype(BF16)
        o_ref[0, :, D_FF + c * LANES:D_FF + (c + 1) * LANES] = r[:, LANES:chunk].astype(BF16)


def _w1_prep_call(w1):
    ne = w1.shape[0]
    perm = np.zeros((2 * LANES, 2 * LANES), np.float32)
    perm[2 * np.arange(LANES), np.arange(LANES)] = 1.0
    perm[2 * np.arange(LANES) + 1, LANES + np.arange(LANES)] = 1.0
    return pl.pallas_call(
        _w1_prep_kernel,
        grid=(ne,),
        in_specs=[pl.BlockSpec((1, D_MODEL, 2 * D_FF), lambda e: (e, 0, 0)),
                  pl.BlockSpec((2 * LANES, 2 * LANES), lambda e: (0, 0))],
        out_specs=pl.BlockSpec((1, D_MODEL, 2 * D_FF), lambda e: (e, 0, 0)),
        out_shape=jax.ShapeDtypeStruct((ne, D_MODEL, 2 * D_FF), BF16),
        compiler_params=_cparams(("arbitrary",)),
    )(w1, jnp.asarray(perm, BF16))


def _expert_kernel(te_ref, src_ref, nxt_ref, h_hbm, w1_ref, b1_ref, w2_ref, b2_ref, o_ref,
                   xa, xb, sem):
    i = pl.program_id(0)
    last = pl.num_programs(0) - 1
    tm = xa.shape[0]

    @pl.when(i == 0)
    def _():
        _start_rows(src_ref, tm, h_hbm, xa, sem.at[0])

    def step(cur, cur_sem, nxt, nxt_sem):
        _wait_rows(tm, h_hbm, cur, cur_sem)
        _start_rows(nxt_ref, tm, h_hbm, nxt, nxt_sem)
        hh = _dot(cur[...].astype(BF16), w1_ref[0]) + b1_ref[0]
        glu = jnp.minimum(hh[:, 0:D_FF], SWIGLU_LIMIT)
        lin = jnp.clip(hh[:, D_FF:2 * D_FF], -SWIGLU_LIMIT, SWIGLU_LIMIT)
        act = glu * _sigmoid(SWIGLU_ALPHA * glu) * (lin + 1.0)
        o_ref[...] = _dot(act.astype(BF16), w2_ref[0]) + b2_ref[0]

        @pl.when(i == last)
        def _():
            _wait_rows(tm, h_hbm, nxt, nxt_sem)

    @pl.when(lax.rem(i, 2) == 0)
    def _():
        step(xa, sem.at[0], xb, sem.at[1])

    @pl.when(lax.rem(i, 2) == 1)
    def _():
        step(xb, sem.at[1], xa, sem.at[0])


def _expert_call(tile_expert, src_rows, h2, w1, b1, w2, b2):
    ntiles = tile_expert.shape[0]
    tm = EXP_TILE
    src3 = src_rows.reshape(ntiles, 1, tm)
    grid_spec = pltpu.PrefetchScalarGridSpec(
        num_scalar_prefetch=1,
        grid=(ntiles,),
        in_specs=[pl.BlockSpec((1, 1, tm), lambda i, te: (i, 0, 0), memory_space=pltpu.SMEM),
                  pl.BlockSpec((1, 1, tm), lambda i, te: (jnp.minimum(i + 1, ntiles - 1), 0, 0),
                               memory_space=pltpu.SMEM),
                  pl.BlockSpec(memory_space=pl.ANY),
                  pl.BlockSpec((1, D_MODEL, 2 * D_FF), lambda i, te: (te[i], 0, 0)),
                  pl.BlockSpec((1, 1, 2 * D_FF), lambda i, te: (te[i], 0, 0)),
                  pl.BlockSpec((1, D_FF, D_MODEL), lambda i, te: (te[i], 0, 0)),
                  pl.BlockSpec((1, 1, D_MODEL), lambda i, te: (te[i], 0, 0))],
        out_specs=pl.BlockSpec((tm, D_MODEL), lambda i, te: (i, 0)),
        scratch_shapes=[pltpu.VMEM((tm, D_MODEL), F32), pltpu.VMEM((tm, D_MODEL), F32),
                        pltpu.SemaphoreType.DMA((2,))],
    )
    return pl.pallas_call(
        _expert_kernel,
        grid_spec=grid_spec,
        out_shape=jax.ShapeDtypeStruct((ntiles * tm, D_MODEL), F32),
        compiler_params=_cparams(("arbitrary",)),
    )(tile_expert, src3, src3, h2, w1, b1, w2, b2)


def _combine_kernel(pos_ref, nxt_ref, y_hbm, wt_ref, x_ref, pg_ref, g_ref, o_ref, ya, yb, sem):
    i = pl.program_id(0) * pl.num_programs(1) + pl.program_id(1)
    last = pl.num_programs(0) * pl.num_programs(1) - 1
    tm = x_ref.shape[0]

    def start(idx_ref, buf, s):
        for k in range(TOP_K):
            _start_rows(idx_ref, tm, y_hbm, buf.at[k], s, idx_off=k * tm)

    def wait(buf, s):
        for k in range(TOP_K):
            _wait_rows(tm, y_hbm, buf.at[k], s)

    @pl.when(i == 0)
    def _():
        start(pos_ref, ya, sem.at[0])

    def step(cur, cur_sem, nxt, nxt_sem):
        wait(cur, cur_sem)
        start(nxt_ref, nxt, nxt_sem)
        wt = wt_ref[...]
        y = wt[:, 0:1] * cur[0]
        for k in range(1, TOP_K):
            y = y + wt[:, k:k + 1] * cur[k]
        ms = jnp.mean(y * y, axis=-1, keepdims=True)
        o_ref[...] = x_ref[...] + g_ref[0] * (y * lax.rsqrt(ms + EPS) * pg_ref[...])

        @pl.when(i == last)
        def _():
            wait(nxt, nxt_sem)

    @pl.when(lax.rem(i, 2) == 0)
    def _():
        step(ya, sem.at[0], yb, sem.at[1])

    @pl.when(lax.rem(i, 2) == 1)
    def _():
        step(yb, sem.at[1], ya, sem.at[0])


def _combine_call(dest, y_sorted, wts, x1, post_g, g2, batch, seq):
    t = x1.shape[0]
    tm = min(CMB_TILE, seq)
    nt = seq // tm
    ntiles = t // tm
    dest_t = dest.reshape(ntiles, tm, TOP_K).transpose(0, 2, 1).reshape(ntiles, 1, TOP_K * tm)
    row = lambda b, i: (b * nt + i, 0)
    return pl.pallas_call(
        _combine_kernel,
        grid=(batch, nt),
        in_specs=[pl.BlockSpec((1, 1, TOP_K * tm), lambda b, i: (b * nt + i, 0, 0),
                               memory_space=pltpu.SMEM),
                  pl.BlockSpec((1, 1, TOP_K * tm),
                               lambda b, i: (jnp.minimum(b * nt + i + 1, ntiles - 1), 0, 0),
                               memory_space=pltpu.SMEM),
                  pl.BlockSpec(memory_space=pl.ANY),
                  pl.BlockSpec((tm, LANES), row),
                  pl.BlockSpec((tm, D_MODEL), row),
                  pl.BlockSpec((1, D_MODEL), lambda b, i: (0, 0)),
                  pl.BlockSpec((1, 1, D_MODEL), lambda b, i: (b, 0, 0))],
        out_specs=pl.BlockSpec((tm, D_MODEL), row),
        out_shape=jax.ShapeDtypeStruct((t, D_MODEL), F32),
        scratch_shapes=[pltpu.VMEM((TOP_K, tm, D_MODEL), F32),
                        pltpu.VMEM((TOP_K, tm, D_MODEL), F32), pltpu.SemaphoreType.DMA((2,))],
        compiler_params=_cparams(("arbitrary", "arbitrary")),
    )(dest_t, dest_t, y_sorted, wts, x1, post_g, g2)


def _dispatch_plan(idx, t):
    tm = EXP_TILE
    npairs = t * TOP_K
    ntiles = npairs // tm + N_EXPERTS
    e_flat = idx.reshape(-1)
    pair = jnp.arange(npairs, dtype=jnp.int32)
    _, order = lax.sort((e_flat, pair), num_keys=1, is_stable=True)
    _, inv = lax.sort((order, pair), num_keys=1, is_stable=True)
    counts = jnp.sum(jax.nn.one_hot(e_flat, N_EXPERTS, dtype=jnp.int32), axis=0)
    cnt_excl = jnp.cumsum(counts) - counts
    tiles_per = (counts + tm - 1) // tm
    tile_end = jnp.cumsum(tiles_per)
    tile_start = tile_end - tiles_per
    tile_ids = jnp.arange(ntiles, dtype=jnp.int32)
    tile_expert = jnp.minimum(
        jnp.sum((tile_ids[:, None] >= tile_end[None, :]).astype(jnp.int32), axis=1),
        N_EXPERTS - 1).astype(jnp.int32)
    rows = jnp.arange(ntiles * tm, dtype=jnp.int32)
    row_e = jnp.repeat(tile_expert, tm)
    off = rows - tile_start[row_e] * tm
    valid = (off < counts[row_e]) & (rows < tile_end[-1] * tm)
    sidx = jnp.clip(cnt_excl[row_e] + off, 0, npairs - 1)
    src_rows = jnp.where(valid, order[sidx] // TOP_K, 0).astype(jnp.int32)
    dest = (tile_start[e_flat] * tm + inv - cnt_excl[e_flat]).astype(jnp.int32)
    return tile_expert, src_rows, dest.reshape(t, TOP_K)


def _rope_tables():
    lane = np.arange(GROUP)
    d = lane % DIFF_D
    rd = DIFF_D // 4
    half = rd // 2
    inv = np.where(d < rd, ROPE_THETA ** (-(d % half).astype(np.float32) / half), 0.0)
    ma = np.where(d < half, -1.0, 0.0)
    mb = np.where((d >= half) & (d < rd), 1.0, 0.0)
    f = lambda a: jnp.asarray(a.reshape(1, GROUP), F32)
    return f(inv), f(ma), f(mb)


def kernel(x, c, positions, ada_w, ada_b, pre_norm_g, post_norm_g, w_in, w_out, fox_fb, hg_lower, hg_norm_g, diff_lam_q1, diff_lam_k1, diff_lam_q2, diff_lam_k2, diff_subln_g, s5_a_re, s5_a_im, s5_log_step, s5_b_re, s5_b_im, s5_c_re, s5_c_im, s5_d, s5_glu_w, router_w, router_b, exp_w1, exp_b1, exp_w2, exp_b2):
    batch, seq, d = x.shape
    depth = ada_w.shape[0]
    t = batch * seq
    assert d == D_MODEL and seq % ROW_TILE == 0

    lb_all = jnp.cumsum(jax.nn.softmax(hg_lower.astype(F32), axis=0), axis=0)
    lb_all = lb_all - lb_all[0:1]
    mod = _ada_call(c, ada_w, ada_b)
    inv_lane, mask_a, mask_b = _rope_tables()
    pos = positions.reshape(t, 1).astype(jnp.int32)
    xf = x.reshape(t, d)

    fcols = 3 * GROUP
    for l in range(depth):
        m6 = mod[l].reshape(batch, 6, 1, d)
        sh1, sc1, g1, sh2, sc2, g2 = [m6[:, j] for j in range(6)]
        wl = w_in[l]
        w_main = jnp.concatenate([wl[:, 0:fcols], wl[:, fcols + FOX_HEADS:]], axis=1).astype(BF16)
        w_f = jnp.pad(wl[:, fcols:fcols + FOX_HEADS], ((0, 0), (0, LANES - FOX_HEADS))).astype(BF16)
        fb = jnp.pad(fox_fb[l], (0, LANES - FOX_HEADS)).reshape(1, LANES)
        fox, frow, hg, diff, s5u = _inproj_call(
            xf, pre_norm_g[l, 0].reshape(1, d), sc1, sh1, w_main, w_f, fb, pos,
            inv_lane, mask_a, mask_b, batch, seq)

        y_fox = _fox_call(fox, frow, batch, seq)

        lam_init = 0.8 - 0.6 * math.exp(-0.3 * l)
        lam = (jnp.exp(jnp.sum(diff_lam_q1[l].astype(F32) * diff_lam_k1[l].astype(F32)))
               - jnp.exp(jnp.sum(diff_lam_q2[l].astype(F32) * diff_lam_k2[l].astype(F32))) + lam_init)
        y_diff = _diff_call(diff, lam.reshape(1), diff_subln_g[l].reshape(1, DIFF_DV),
                            1.0 - lam_init, batch, seq)

        y_hg = _hgrn2_call(hg, lb_all[l].reshape(1, -1), hg_norm_g[l].reshape(1, HG_DV), batch, seq)

        bbar, ar, ai, cmat = _s5_params(s5_a_re[l], s5_a_im[l], s5_log_step[l], s5_b_re[l],
                                        s5_b_im[l], s5_c_re[l], s5_c_im[l])
        u_sb = s5u.reshape(batch, seq, GROUP).transpose(1, 0, 2).reshape(t, GROUP)
        y_s5_sb = _s5_call(u_sb, bbar, ar, ai, cmat, s5_d[l].reshape(1, GROUP),
                           s5_glu_w[l].astype(BF16), batch, seq)
        y_s5 = y_s5_sb.reshape(seq, batch, GROUP).transpose(1, 0, 2).reshape(t, GROUP)

        rw = jnp.pad(router_w[l], ((0, 0), (0, LANES - N_EXPERTS)))
        rb = jnp.pad(router_b[l], (0, LANES - N_EXPERTS), constant_values=NEG_BIG).reshape(1, LANES)
        x1, h2, idx, wts = _outproj_call(
            y_fox, y_hg, y_diff, y_s5, w_out[l].astype(BF16), xf,
            post_norm_g[l, 0].reshape(1, d), g1, pre_norm_g[l, 1].reshape(1, d), sc2, sh2,
            rw, rb, batch, seq)

        tile_expert, src_rows, dest = _dispatch_plan(idx[:, 0:TOP_K], t)
        w1p = _w1_prep_call(exp_w1[l])
        b1 = exp_b1[l]
        b1p = jnp.concatenate([b1[:, 0::2], b1[:, 1::2]], axis=-1).reshape(N_EXPERTS, 1, 2 * D_FF)
        y_sorted = _expert_call(tile_expert, src_rows, h2, w1p, b1p, exp_w2[l].astype(BF16),
                                exp_b2[l].reshape(N_EXPERTS, 1, D_MODEL))
        xf = _combine_call(dest, y_sorted, wts, x1, post_norm_g[l, 1].reshape(1, d), g2, batch, seq)

    return xf.reshape(batch, seq, d)
```

```python
import functools
import math

import numpy as np
import jax
import jax.numpy as jnp
from jax import lax
from jax.experimental import pallas as pl
from jax.experimental.pallas import tpu as pltpu

F32 = jnp.float32
BF16 = jnp.bfloat16

D_MODEL = 1024
GROUP = 256
FOX_HEADS = 4
FOX_HD = 64
HG_HEADS = 4
HG_DK = 128
HG_DV = 64
DIFF_HEADS = 4
DIFF_DV = 64
DIFF_D = 32
S5_CH = 16
S5_GROUPS = 16
S5_N = 64
ROPE_THETA = 500000.0
N_EXPERTS = 32
TOP_K = 4
D_FF = 1024
SWIGLU_LIMIT = 7.0
SWIGLU_ALPHA = 1.702
EPS = 1e-6
NEG_BIG = -1e30

LANES = 128
VMEM_LIMIT = 56 * 1024 * 1024

LOG2E = 1.4426950408889634

ROW_TILE = 512
ATT_T = ROW_TILE
HG_CHUNK = 64
S5_STEPS = 128
EXP_TILE = 256
CMB_TILE = 256

ROW_SPLIT = D_MODEL // LANES

HG_COLS = 2 * HG_HEADS * HG_DK + 2 * GROUP
MAIN_COLS = 3 * GROUP + HG_COLS + 3 * GROUP + GROUP


def _cparams(sem):
    return pltpu.CompilerParams(dimension_semantics=sem, vmem_limit_bytes=VMEM_LIMIT)


def _sigmoid(x):
    return 1.0 / (1.0 + jnp.exp(-x))


def _shr(x, pow2):
    return lax.shift_right_logical(x, int(math.log2(pow2)))


def _dot(a, b):
    return jnp.dot(a, b, preferred_element_type=F32)


def _dot_nt(a, b):
    return lax.dot_general(a, b, (((1,), (1,)), ((), ())), preferred_element_type=F32)


def _dot_tn(a, b):
    return lax.dot_general(a, b, (((0,), (0,)), ((), ())), preferred_element_type=F32)


def _split_dot(m_bf16, x):
    hi = x.astype(BF16)
    r1 = x - hi.astype(F32)
    mid = r1.astype(BF16)
    lo = (r1 - mid.astype(F32)).astype(BF16)
    return _dot(m_bf16, hi) + _dot(m_bf16, mid) + _dot(m_bf16, lo)


def _ada_kernel(c_ref, w_ref, b_ref, o_ref):
    c = c_ref[...]
    cs = c * _sigmoid(c)
    o_ref[0] = jnp.dot(cs, w_ref[0], preferred_element_type=F32,
                       precision=lax.Precision.HIGHEST) + b_ref[0]


def _ada_call(c, ada_w, ada_b):
    depth, d, n = ada_w.shape
    b = c.shape[0]
    tn = 1536
    return pl.pallas_call(
        _ada_kernel,
        grid=(depth, n // tn),
        in_specs=[pl.BlockSpec((b, d), lambda l, j: (0, 0)),
                  pl.BlockSpec((1, d, tn), lambda l, j: (l, 0, j)),
                  pl.BlockSpec((1, 1, tn), lambda l, j: (l, 0, j))],
        out_specs=pl.BlockSpec((1, b, tn), lambda l, j: (l, 0, j)),
        out_shape=jax.ShapeDtypeStruct((depth, b, n), F32),
        compiler_params=_cparams(("arbitrary", "arbitrary")),
    )(c, ada_w, ada_b.reshape(depth, 1, n))


def _inproj_kernel(x_ref, g_ref, sc_ref, sh_ref, w_ref, wf_ref, fb_ref, pos_ref, inv_ref,
                   ma_ref, mb_ref,
                   fox_ref, frow_ref, hg_ref, diff_ref, s5_ref, carry_ref):
    i = pl.program_id(1)
    tm = x_ref.shape[0]

    @pl.when(i == 0)
    def _():
        carry_ref[...] = jnp.zeros_like(carry_ref)

    x = x_ref[...]
    ms = jnp.mean(x * x, axis=-1, keepdims=True)
    h = (x * lax.rsqrt(ms + EPS) * g_ref[...]) * (1.0 + sc_ref[0]) + sh_ref[0]
    hb = h.astype(BF16)

    o = 0
    pf = _dot(hb, w_ref[:, o:o + 3 * GROUP])
    fox_ref[:, 0:GROUP] = (pf[:, 0:GROUP] * (FOX_HD ** -0.5 * LOG2E)).astype(BF16)
    fox_ref[:, GROUP:3 * GROUP] = pf[:, GROUP:3 * GROUP].astype(BF16)
    o += 3 * GROUP

    ff = _dot(hb, wf_ref[...]) + fb_ref[...]
    logf = jnp.minimum(ff, 0.0) - jnp.log(1.0 + jnp.exp(-jnp.abs(ff)))
    r = lax.broadcasted_iota(jnp.int32, (tm, tm), 0)
    cidx = lax.broadcasted_iota(jnp.int32, (tm, tm), 1)
    tri = jnp.where(cidx <= r, 1.0, 0.0).astype(BF16)
    cum = _split_dot(tri, logf) + carry_ref[...]
    carry_ref[...] = cum[tm - 1:tm, :]
    cum_t = jnp.transpose(cum * LOG2E)
    frow_ref[0, 0] = cum_t[0:8, :]

    hg_ref[...] = _dot(hb, w_ref[:, o:o + HG_COLS]).astype(BF16)
    o += HG_COLS

    pd = _dot(hb, w_ref[:, o:o + 3 * GROUP])
    o += 3 * GROUP
    pos = pos_ref[...].astype(F32)
    ang = pos * inv_ref[...]
    cos = jnp.cos(ang)
    sin = jnp.sin(ang)
    ma = ma_ref[...]
    mb = mb_ref[...]

    def rope(xx):
        outs = []
        for j in range(GROUP // LANES):
            sl = slice(j * LANES, (j + 1) * LANES)
            xs = xx[:, sl]
            partner = (pltpu.roll(xs, LANES - 4, axis=1) * ma[:, sl]
                       + pltpu.roll(xs, 4, axis=1) * mb[:, sl])
            outs.append(xs * cos[:, sl] + partner * sin[:, sl])
        return jnp.concatenate(outs, axis=1)

    diff_ref[:, 0:GROUP] = (rope(pd[:, 0:GROUP]) * (DIFF_D ** -0.5 * LOG2E)).astype(BF16)
    diff_ref[:, GROUP:2 * GROUP] = rope(pd[:, GROUP:2 * GROUP]).astype(BF16)
    diff_ref[:, 2 * GROUP:3 * GROUP] = pd[:, 2 * GROUP:3 * GROUP].astype(BF16)

    s5_ref[...] = _dot(hb, w_ref[:, o:o + GROUP]).astype(BF16)


def _inproj_call(x2, g_pre, sc, sh, w_main, w_f, fb, pos, inv_lane, mask_a, mask_b, batch, seq):
    t = x2.shape[0]
    tm = ROW_TILE
    nt = seq // tm
    row = lambda b, i: (b * nt + i, 0)
    const2 = lambda b, i: (0, 0)
    per_b = lambda b, i: (b, 0, 0)
    outs = pl.pallas_call(
        _inproj_kernel,
        grid=(batch, nt),
        in_specs=[pl.BlockSpec((tm, D_MODEL), row),
                  pl.BlockSpec((1, D_MODEL), const2),
                  pl.BlockSpec((1, 1, D_MODEL), per_b),
                  pl.BlockSpec((1, 1, D_MODEL), per_b),
                  pl.BlockSpec((D_MODEL, MAIN_COLS), const2),
                  pl.BlockSpec((D_MODEL, LANES), const2),
                  pl.BlockSpec((1, LANES), const2),
                  pl.BlockSpec((tm, 1), row),
                  pl.BlockSpec((1, GROUP), const2),
                  pl.BlockSpec((1, GROUP), const2),
                  pl.BlockSpec((1, GROUP), const2)],
        out_specs=[pl.BlockSpec((tm, 3 * GROUP), row),
                   pl.BlockSpec((1, 1, 8, tm), lambda b, i: (b, i, 0, 0)),
                   pl.BlockSpec((tm, HG_COLS), row),
                   pl.BlockSpec((tm, 3 * GROUP), row),
                   pl.BlockSpec((tm, GROUP), row)],
        out_shape=[jax.ShapeDtypeStruct((t, 3 * GROUP), BF16),
                   jax.ShapeDtypeStruct((batch, nt, 8, tm), F32),
                   jax.ShapeDtypeStruct((t, HG_COLS), BF16),
                   jax.ShapeDtypeStruct((t, 3 * GROUP), BF16),
                   jax.ShapeDtypeStruct((t, GROUP), BF16)],
        scratch_shapes=[pltpu.VMEM((1, LANES), F32)],
        compiler_params=_cparams(("arbitrary", "arbitrary")),
    )(x2, g_pre, sc, sh, w_main, w_f, fb, pos, inv_lane, mask_a, mask_b)
    return outs


_HEAD_W = 64
_HEADS = 4


def _slot(x64):
    return jnp.concatenate([x64, jnp.zeros_like(x64)], axis=1)


def _pad_kv(k_ref, v_ref, k_s, v_s):
    seq = k_ref.shape[0]
    rows = ATT_T
    lane = lax.broadcasted_iota(jnp.int32, (rows, LANES), 1)

    def body(c, _):
        r0 = pl.multiple_of(c * rows, rows)
        for h in range(_HEADS):
            k_s[pl.ds(r0, rows), h * LANES:(h + 1) * LANES] = _slot(
                k_ref[pl.ds(r0, rows), h * _HEAD_W:(h + 1) * _HEAD_W])
            v = _slot(v_ref[pl.ds(r0, rows), h * _HEAD_W:(h + 1) * _HEAD_W])
            v_s[pl.ds(r0, rows), h * LANES:(h + 1) * LANES] = jnp.where(
                lane == _HEAD_W, jnp.ones_like(v), v)
        return 0

    lax.fori_loop(0, seq // rows, body, 0)


def _attend(qs, slots, k_s, v_s, qi, key_bias):
    n = len(qs)
    tq = qs[0].shape[0]
    tk = ATT_T

    def block(kb, carry, masked):
        ms, accs = carry
        start = pl.multiple_of(kb * tk, tk)
        if masked:
            rr = lax.broadcasted_iota(jnp.int32, (tq, tk), 0)
            cc = lax.broadcasted_iota(jnp.int32, (tq, tk), 1)
            keep = cc <= rr
        new_m, new_acc = [], []
        for j in range(n):
            c0 = slots[j] * LANES
            s = _dot_nt(qs[j], k_s[pl.ds(start, tk), c0:c0 + LANES])
            kbias = key_bias(j, kb)
            if kbias is not None:
                s = s - kbias
            if masked:
                s = jnp.where(keep, s, NEG_BIG)
            m_new = jnp.maximum(ms[j], jnp.max(s, axis=-1, keepdims=True))
            alpha = jnp.exp2(ms[j] - m_new)
            p = jnp.exp2(s - m_new).astype(BF16)
            new_acc.append(alpha * accs[j] + _dot(p, v_s[pl.ds(start, tk), c0:c0 + LANES]))
            new_m.append(m_new)
        return tuple(new_m), tuple(new_acc)

    init = (tuple(jnp.full((tq, 1), NEG_BIG, F32) for _ in range(n)),
            tuple(jnp.zeros((tq, LANES), F32) for _ in range(n)))
    carry = lax.fori_loop(0, qi, lambda kb, c: block(kb, c, False), init)
    _, accs = block(qi, carry, True)
    return accs


def _fox_kernel(q_ref, k_ref, v_ref, frow_ref, o_ref, k_s, v_s):
    qi = pl.program_id(1)

    @pl.when(qi == 0)
    def _():
        _pad_kv(k_ref, v_ref, k_s, v_s)

    qs = [_slot(q_ref[:, h * _HEAD_W:(h + 1) * _HEAD_W]) for h in range(_HEADS)]
    accs = _attend(qs, list(range(_HEADS)), k_s, v_s, qi,
                   lambda j, kb: frow_ref[0, kb, j:j + 1, :])
    outs = [a[:, 0:_HEAD_W] / a[:, _HEAD_W:_HEAD_W + 1] for a in accs]
    o_ref[...] = jnp.concatenate(outs, axis=1).astype(BF16)


def _att_specs(seq):
    nq = seq // ATT_T
    return ([pl.BlockSpec((ATT_T, GROUP), lambda b, i: (b * nq + i, 0)),
             pl.BlockSpec((seq, GROUP), lambda b, i: (b, 1)),
             pl.BlockSpec((seq, GROUP), lambda b, i: (b, 2))],
            pl.BlockSpec((ATT_T, GROUP), lambda b, i: (b * nq + i, 0)),
            [pltpu.VMEM((seq, _HEADS * LANES), BF16), pltpu.VMEM((seq, _HEADS * LANES), BF16)])


def _fox_call(fox, frow, batch, seq):
    t = fox.shape[0]
    nq = seq // ATT_T
    qkv_specs, out_spec, scratch = _att_specs(seq)
    return pl.pallas_call(
        _fox_kernel,
        grid=(batch, nq),
        in_specs=qkv_specs + [pl.BlockSpec((1, nq, 8, ATT_T), lambda b, i: (b, 0, 0, 0))],
        out_specs=out_spec,
        out_shape=jax.ShapeDtypeStruct((t, GROUP), BF16),
        scratch_shapes=scratch,
        compiler_params=_cparams(("arbitrary", "arbitrary")),
    )(fox, fox, fox, frow)


def _diff_kernel(lam_ref, q_ref, k_ref, v_ref, g_ref, o_ref, k_s, v_s, *, out_scale):
    qi = pl.program_id(1)

    @pl.when(qi == 0)
    def _():
        _pad_kv(k_ref, v_ref, k_s, v_s)

    lam = lam_ref[0]
    lane = lax.broadcasted_iota(jnp.int32, (ATT_T, LANES), 1)
    outs = []
    for pair in range(_HEADS // 2):
        qs, slots = [], []
        for h in (2 * pair, 2 * pair + 1):
            q = _slot(q_ref[:, h * _HEAD_W:(h + 1) * _HEAD_W])
            qs += [jnp.where(lane < DIFF_D, q, jnp.zeros_like(q)),
                   jnp.where(lane >= DIFF_D, q, jnp.zeros_like(q))]
            slots += [h, h]
        accs = _attend(qs, slots, k_s, v_s, qi, lambda j, kb: None)
        for j in (0, 2):
            y = (accs[j][:, 0:_HEAD_W] / accs[j][:, _HEAD_W:_HEAD_W + 1]
                 - lam * (accs[j + 1][:, 0:_HEAD_W] / accs[j + 1][:, _HEAD_W:_HEAD_W + 1]))
            ms = jnp.mean(y * y, axis=-1, keepdims=True)
            outs.append(y * lax.rsqrt(ms + EPS) * g_ref[...] * out_scale)
    o_ref[...] = jnp.concatenate(outs, axis=1).astype(BF16)


def _diff_call(diff, lam, subln_g, out_scale, batch, seq):
    t = diff.shape[0]
    nq = seq // ATT_T
    qkv_specs, out_spec, scratch = _att_specs(seq)
    return pl.pallas_call(
        functools.partial(_diff_kernel, out_scale=out_scale),
        grid=(batch, nq),
        in_specs=([pl.BlockSpec(memory_space=pltpu.SMEM)] + qkv_specs
                  + [pl.BlockSpec((1, DIFF_DV), lambda b, i: (0, 0))]),
        out_specs=out_spec,
        out_shape=jax.ShapeDtypeStruct((t, GROUP), BF16),
        scratch_shapes=scratch,
        compiler_params=_cparams(("arbitrary", "arbitrary")),
    )(lam, diff, diff, diff, subln_g)


_HG_LEVELS = (32, 16, 8)


def _hg_ref_rows(G, level):
    C = G.shape[0]
    rows = []
    for tile in range(C // 8):
        if level is None:
            r = tile * 8 + 3
        else:
            grp = (tile * 8) // (2 * level)
            r = grp * 2 * level + level - 1
        rows.append(jnp.broadcast_to(G[r:r + 1, :], (8, G.shape[1])))
    return jnp.concatenate(rows, axis=0)


def _hgrn2_kernel(p_ref, lb_ref, ng_ref, o_ref, q_s, g_s, k_s, state_ref):
    i = pl.program_id(1)
    tm = p_ref.shape[0]
    C = HG_CHUNK
    KW = HG_HEADS * HG_DK

    @pl.when(i == 0)
    def _():
        state_ref[...] = jnp.zeros_like(state_ref)

    hq = p_ref[:, 0:KW].astype(F32)
    q_s[...] = hq * _sigmoid(hq)
    lb = lb_ref[...]
    f = lb + (1.0 - lb) * _sigmoid(p_ref[:, KW:2 * KW].astype(F32))
    k_s[...] = 1.0 - f
    r = lax.broadcasted_iota(jnp.int32, (tm, tm), 0)
    c = lax.broadcasted_iota(jnp.int32, (tm, tm), 1)
    bd = jnp.where((c <= r) & (_shr(r, C) == _shr(c, C)), 1.0, 0.0).astype(BF16)
    g_s[...] = _split_dot(bd, jnp.log(f))

    rr = lax.broadcasted_iota(jnp.int32, (C, C), 0)
    cc = lax.broadcasted_iota(jnp.int32, (C, C), 1)
    level_masks = []
    for m in _HG_LEVELS:
        same_grp = _shr(rr, 2 * m) == _shr(cc, 2 * m)
        level_masks.append(same_grp & ((rr & (2 * m - 1)) >= m) & ((cc & (2 * m - 1)) < m))
    diag_mask = (_shr(rr, 8) == _shr(cc, 8)) & (cc <= rr)
    row_idx = lax.broadcasted_iota(jnp.int32, (C, HG_DK), 0)

    def chunk(ci, _):
        base = pl.multiple_of(ci * C, C)
        for h in range(HG_HEADS):
            ks = slice(h * HG_DK, (h + 1) * HG_DK)
            q = q_s[pl.ds(base, C), ks]
            k = k_s[pl.ds(base, C), ks]
            G = g_s[pl.ds(base, C), ks]
            v = p_ref[pl.ds(base, C), 2 * KW + h * HG_DV:2 * KW + (h + 1) * HG_DV]
            st = state_ref[h]
            gl = G[C - 1:C, :]
            o = _dot_nt((q * jnp.exp(G)).astype(BF16), st.astype(BF16))
            a = jnp.zeros((C, C), F32)
            for m, mask in zip(_HG_LEVELS, level_masks):
                ref = _hg_ref_rows(G, m)
                upper = (row_idx & (2 * m - 1)) >= m
                e = jnp.exp(jnp.where(upper, G - ref, ref - G))
                am = _dot_nt((q * e).astype(BF16), (k * e).astype(BF16))
                a = a + jnp.where(mask, am, 0.0)
            ref = _hg_ref_rows(G, None)
            d = jnp.clip(G - ref, -80.0, 80.0)
            am = _dot_nt((q * jnp.exp(d)).astype(BF16), (k * jnp.exp(-d)).astype(BF16))
            a = a + jnp.where(diag_mask, am, 0.0)
            o = o + _dot(a.astype(BF16), v)
            kd = (k * jnp.exp(gl - G)).astype(BF16)
            state_ref[h] = jnp.exp(gl) * st + _dot_tn(v, kd)
            ms = jnp.mean(o * o, axis=-1, keepdims=True)
            on = o * lax.rsqrt(ms + EPS) * ng_ref[...]
            gate = p_ref[pl.ds(base, C), 2 * KW + GROUP + h * HG_DV:
                         2 * KW + GROUP + (h + 1) * HG_DV].astype(F32)
            o_ref[pl.ds(base, C), h * HG_DV:(h + 1) * HG_DV] = (
                on * (gate * _sigmoid(gate))).astype(BF16)
        return 0

    lax.fori_loop(0, tm // C, chunk, 0)


def _hgrn2_call(hg, lb, norm_g, batch, seq):
    t = hg.shape[0]
    tm = min(ROW_TILE, seq)
    nt = seq // tm
    kw = HG_HEADS * HG_DK
    return pl.pallas_call(
        _hgrn2_kernel,
        grid=(batch, nt),
        in_specs=[pl.BlockSpec((tm, HG_COLS), lambda b, i: (b * nt + i, 0)),
                  pl.BlockSpec((1, kw), lambda b, i: (0, 0)),
                  pl.BlockSpec((1, HG_DV), lambda b, i: (0, 0))],
        out_specs=pl.BlockSpec((tm, GROUP), lambda b, i: (b * nt + i, 0)),
        out_shape=jax.ShapeDtypeStruct((t, GROUP), BF16),
        scratch_shapes=[pltpu.VMEM((tm, kw), F32), pltpu.VMEM((tm, kw), F32),
                        pltpu.VMEM((tm, kw), F32),
                        pltpu.VMEM((HG_HEADS, HG_DV, HG_DK), F32)],
        compiler_params=_cparams(("arbitrary", "arbitrary")),
    )(hg, lb, norm_g)


def _s5_kernel(u_ref, bb_ref, ar_ref, ai_ref, cm_ref, d_ref, glu_ref, o_ref, bu_s, xr_s, xi_s,
               *, batch):
    i = pl.program_id(0)
    nst = S5_GROUPS * S5_N
    steps = u_ref.shape[0] // batch

    @pl.when(i == 0)
    def _():
        xr_s[...] = jnp.zeros_like(xr_s)
        xi_s[...] = jnp.zeros_like(xi_s)

    u = u_ref[...]
    bu_s[...] = _dot(u, bb_ref[...])
    ar = jnp.broadcast_to(ar_ref[...], (batch, nst))
    ai = jnp.broadcast_to(ai_ref[...], (batch, nst))

    def step(t, carry):
        xr, xi = carry
        r0 = pl.multiple_of(t * batch, batch)
        nr = ar * xr - ai * xi + bu_s[pl.ds(r0, batch), 0:nst]
        ni = ar * xi + ai * xr + bu_s[pl.ds(r0, batch), nst:2 * nst]
        bu_s[pl.ds(r0, batch), 0:nst] = nr
        bu_s[pl.ds(r0, batch), nst:2 * nst] = ni
        return nr, ni

    xr, xi = lax.fori_loop(0, steps, step, (xr_s[...], xi_s[...]))
    xr_s[...] = xr
    xi_s[...] = xi
    y = _dot(bu_s[...].astype(BF16), cm_ref[...]) + d_ref[...] * u.astype(F32)
    gel = 0.5 * y * (1.0 + jnp.tanh(0.7978845608028654 * (y + 0.044715 * (y * y * y))))
    z = _dot(gel.astype(BF16), glu_ref[...])
    o_ref[...] = (z[:, 0:GROUP] * _sigmoid(z[:, GROUP:2 * GROUP])).astype(BF16)


def _s5_call(u_sb, bbar, ar, ai, cmat, dskip, glu_w, batch, seq):
    rows = u_sb.shape[0]
    steps = min(S5_STEPS, seq)
    tr = steps * batch
    nst = S5_GROUPS * S5_N
    const = lambda i: (0, 0)
    return pl.pallas_call(
        functools.partial(_s5_kernel, batch=batch),
        grid=(seq // steps,),
        in_specs=[pl.BlockSpec((tr, GROUP), lambda i: (i, 0)),
                  pl.BlockSpec((GROUP, 2 * nst), const),
                  pl.BlockSpec((1, nst), const),
                  pl.BlockSpec((1, nst), const),
                  pl.BlockSpec((2 * nst, GROUP), const),
                  pl.BlockSpec((1, GROUP), const),
                  pl.BlockSpec((GROUP, 2 * GROUP), const)],
        out_specs=pl.BlockSpec((tr, GROUP), lambda i: (i, 0)),
        out_shape=jax.ShapeDtypeStruct((rows, GROUP), BF16),
        scratch_shapes=[pltpu.VMEM((tr, 2 * nst), F32),
                        pltpu.VMEM((batch, nst), F32), pltpu.VMEM((batch, nst), F32)],
        compiler_params=_cparams(("arbitrary",)),
    )(u_sb, bbar, ar, ai, cmat, dskip, glu_w)


def _s5_params(a_re, a_im, log_step, b_re, b_im, c_re, c_im):
    lr = jnp.minimum(a_re.astype(F32), -1e-4)
    li = a_im.astype(F32)
    dt = jnp.exp(log_step.astype(F32))[:, None]
    mag = jnp.exp(lr * dt)
    ar = mag * jnp.cos(li * dt)
    ai = mag * jnp.sin(li * dt)
    den = lr * lr + li * li
    zr = ((ar - 1.0) * lr + ai * li) / den
    zi = (ai * lr - (ar - 1.0) * li) / den
    br = b_re.astype(F32)
    bi = b_im.astype(F32)
    bbr = zr[..., None] * br - zi[..., None] * bi
    bbi = zr[..., None] * bi + zi[..., None] * br
    eye = jnp.eye(S5_GROUPS, dtype=F32)
    def bd_in(m):
        return jnp.einsum('gnh,gk->ghkn', m, eye).reshape(GROUP, S5_GROUPS * S5_N)
    def bd_out(m):
        return jnp.einsum('ghn,gk->gnkh', m, eye).reshape(S5_GROUPS * S5_N, GROUP)
    bbar = jnp.concatenate([bd_in(bbr), bd_in(bbi)], axis=1).astype(BF16)
    cmat = jnp.concatenate([bd_out(c_re.astype(F32)), -bd_out(c_im.astype(F32))],
                           axis=0).astype(BF16)
    return bbar, ar.reshape(1, -1), ai.reshape(1, -1), cmat


def _outproj_kernel(yf_ref, yh_ref, yd_ref, ys_ref, wo_ref, x_ref, pg_ref, g1_ref, g2_ref,
                    sc_ref, sh_ref, rw_ref, rb_ref,
                    x1_ref, h2_ref, idx_ref, wt_ref, cnt_ref, carry_ref):
    tm = x_ref.shape[0]

    @pl.when((pl.program_id(0) == 0) & (pl.program_id(1) == 0))
    def _():
        carry_ref[...] = jnp.zeros_like(carry_ref)

    y = (_dot(yf_ref[...], wo_ref[0:GROUP, :])
         + _dot(yh_ref[...], wo_ref[GROUP:2 * GROUP, :])
         + _dot(yd_ref[...], wo_ref[2 * GROUP:3 * GROUP, :])
         + _dot(ys_ref[...], wo_ref[3 * GROUP:4 * GROUP, :]))
    ms = jnp.mean(y * y, axis=-1, keepdims=True)
    x1 = x_ref[...] + g1_ref[0] * (y * lax.rsqrt(ms + EPS) * pg_ref[...])
    x1_ref[...] = x1
    ms2 = jnp.mean(x1 * x1, axis=-1, keepdims=True)
    h2 = (x1 * lax.rsqrt(ms2 + EPS) * g2_ref[...]) * (1.0 + sc_ref[0]) + sh_ref[0]
    for c in range(ROW_SPLIT):
        h2_ref[pl.ds(c, tm, stride=ROW_SPLIT), :] = h2[:, c * LANES:(c + 1) * LANES]
    h_hi = h2.astype(BF16)
    h_lo = (h2 - h_hi.astype(F32)).astype(BF16)
    rw = rw_ref[...]
    w_hi = rw.astype(BF16)
    w_lo = (rw - w_hi.astype(F32)).astype(BF16)
    logits = _dot(h_hi, w_hi) + (_dot(h_lo, w_hi) + _dot(h_hi, w_lo)) + rb_ref[...]
    lane = lax.broadcasted_iota(jnp.int32, (tm, LANES), 1)
    cur = logits
    vals, idxs = [], []
    for _ in range(TOP_K):
        m = jnp.max(cur, axis=-1, keepdims=True)
        sel = jnp.min(jnp.where(cur == m, lane, LANES), axis=-1, keepdims=True)
        vals.append(m)
        idxs.append(sel)
        cur = jnp.where(lane == sel, -jnp.inf, cur)
    es = [jnp.exp(v - vals[0]) for v in vals]
    tot = es[0] + es[1] + es[2] + es[3]
    hits = (lane == idxs[0]) | (lane == idxs[1]) | (lane == idxs[2]) | (lane == idxs[3])
    cnt = jnp.where(hits, 1.0, 0.0)
    r = lax.broadcasted_iota(jnp.int32, (tm, tm), 0)
    cidx = lax.broadcasted_iota(jnp.int32, (tm, tm), 1)
    below = jnp.where(cidx < r, 1.0, 0.0).astype(BF16)
    before = _dot(below, cnt.astype(BF16)) + carry_ref[...]
    total = carry_ref[...] + jnp.sum(cnt, axis=0, keepdims=True)
    carry_ref[...] = total
    cnt_ref[...] = total
    idx_out = jnp.zeros((tm, LANES), jnp.int32)
    wt_out = jnp.zeros((tm, LANES), F32)
    for k in range(TOP_K):
        rank = jnp.sum(jnp.where(lane == idxs[k], before, 0.0), axis=-1, keepdims=True)
        idx_out = jnp.where(lane == k, idxs[k], idx_out)
        idx_out = jnp.where(lane == TOP_K + k, rank.astype(jnp.int32), idx_out)
        wt_out = jnp.where(lane == k, es[k] / tot, wt_out)
    idx_ref[...] = idx_out
    wt_ref[...] = wt_out


def _outproj_call(yf, yh, yd, ys, w_out, x2, post_g, g1, pre_g2, sc2, sh2, rw, rb, batch, seq):
    t = x2.shape[0]
    tm = min(ROW_TILE, seq)
    nt = seq // tm
    row = lambda b, i: (b * nt + i, 0)
    const2 = lambda b, i: (0, 0)
    per_b = lambda b, i: (b, 0, 0)
    grp = pl.BlockSpec((tm, GROUP), row)
    vec = pl.BlockSpec((1, D_MODEL), const2)
    mod = pl.BlockSpec((1, 1, D_MODEL), per_b)
    return pl.pallas_call(
        _outproj_kernel,
        grid=(batch, nt),
        in_specs=[grp, grp, grp, grp,
                  pl.BlockSpec((D_MODEL, D_MODEL), const2),
                  pl.BlockSpec((tm, D_MODEL), row),
                  vec, mod, vec, mod, mod,
                  pl.BlockSpec((D_MODEL, LANES), const2),
                  pl.BlockSpec((1, LANES), const2)],
        out_specs=[pl.BlockSpec((tm, D_MODEL), row),
                   pl.BlockSpec((tm * ROW_SPLIT, LANES), row),
                   pl.BlockSpec((tm, LANES), row), pl.BlockSpec((tm, LANES), row),
                   pl.BlockSpec((1, LANES), const2)],
        out_shape=[jax.ShapeDtypeStruct((t, D_MODEL), F32),
                   jax.ShapeDtypeStruct((t * ROW_SPLIT, LANES), F32),
                   jax.ShapeDtypeStruct((t, LANES), jnp.int32),
                   jax.ShapeDtypeStruct((t, LANES), F32),
                   jax.ShapeDtypeStruct((1, LANES), F32)],
        scratch_shapes=[pltpu.VMEM((1, LANES), F32)],
        compiler_params=_cparams(("arbitrary", "arbitrary")),
    )(yf, yh, yd, ys, w_out, x2, post_g, g1, pre_g2, sc2, sh2, rw, rb)


def _row_copy(src_hbm, dst_vmem, src_row, dst_row, sem):
    s0 = pl.multiple_of(src_row * ROW_SPLIT, ROW_SPLIT)
    return pltpu.make_async_copy(src_hbm.at[pl.ds(s0, ROW_SPLIT), :],
                                 dst_vmem.at[pl.ds(dst_row * ROW_SPLIT, ROW_SPLIT), :], sem)


def _start_rows(idx_ref, n, src_hbm, dst_vmem, sem, idx_off=0):
    for j in range(n):
        _row_copy(src_hbm, dst_vmem, idx_ref[0, 0, idx_off + j], j, sem).start()


def _wait_rows(n, src_hbm, dst_vmem, sem):
    for j in range(n):
        _row_copy(src_hbm, dst_vmem, 0, j, sem).wait()


def _tile_rows(buf, n):
    return jnp.concatenate([buf[pl.ds(c, n, stride=ROW_SPLIT), :] for c in range(ROW_SPLIT)],
                           axis=1)


def _expert_kernel(te_ref, src_ref, nxt_ref, h_hbm, w1_ref, p_ref, b1_ref, w2_ref, b2_ref,
                   o_ref, xa, xb, w1_s, w2_s, sem):
    i = pl.program_id(0)
    last = pl.num_programs(0) - 1
    tm = EXP_TILE

    @pl.when(i == 0)
    def _():
        _start_rows(src_ref, tm, h_hbm, xa, sem.at[0])

    @pl.when((i == 0) | (te_ref[i] != te_ref[jnp.maximum(i - 1, 0)]))
    def _():
        chunk = 2 * LANES
        for c in range(2 * D_FF // chunk):
            r = _dot(w1_ref[0, :, c * chunk:(c + 1) * chunk].astype(BF16), p_ref[...])
            w1_s[:, c * LANES:(c + 1) * LANES] = r[:, 0:LANES].astype(BF16)
            w1_s[:, D_FF + c * LANES:D_FF + (c + 1) * LANES] = r[:, LANES:chunk].astype(BF16)
        w2_s[...] = w2_ref[0].astype(BF16)

    def step(cur, cur_sem, nxt, nxt_sem):
        _wait_rows(tm, h_hbm, cur, cur_sem)
        _start_rows(nxt_ref, tm, h_hbm, nxt, nxt_sem)
        hh = _dot(_tile_rows(cur, tm).astype(BF16), w1_s[...]) + b1_ref[0]
        glu = jnp.minimum(hh[:, 0:D_FF], SWIGLU_LIMIT)
        lin = jnp.clip(hh[:, D_FF:2 * D_FF], -SWIGLU_LIMIT, SWIGLU_LIMIT)
        act = glu * _sigmoid(SWIGLU_ALPHA * glu) * (lin + 1.0)
        y = _dot(act.astype(BF16), w2_s[...]) + b2_ref[0]
        for c in range(ROW_SPLIT):
            o_ref[pl.ds(c, tm, stride=ROW_SPLIT), :] = y[:, c * LANES:(c + 1) * LANES]

        @pl.when(i == last)
        def _():
            _wait_rows(tm, h_hbm, nxt, nxt_sem)

    @pl.when(lax.rem(i, 2) == 0)
    def _():
        step(xa, sem.at[0], xb, sem.at[1])

    @pl.when(lax.rem(i, 2) == 1)
    def _():
        step(xb, sem.at[1], xa, sem.at[0])


def _expert_call(tile_expert, src_rows, h2, w1, b1, w2, b2):
    ntiles = tile_expert.shape[0]
    tm = EXP_TILE
    src3 = src_rows.reshape(ntiles, 1, tm)
    perm = np.zeros((2 * LANES, 2 * LANES), np.float32)
    perm[2 * np.arange(LANES), np.arange(LANES)] = 1.0
    perm[2 * np.arange(LANES) + 1, LANES + np.arange(LANES)] = 1.0
    grid_spec = pltpu.PrefetchScalarGridSpec(
        num_scalar_prefetch=1,
        grid=(ntiles,),
        in_specs=[pl.BlockSpec((1, 1, tm), lambda i, te: (i, 0, 0), memory_space=pltpu.SMEM),
                  pl.BlockSpec((1, 1, tm), lambda i, te: (jnp.minimum(i + 1, ntiles - 1), 0, 0),
                               memory_space=pltpu.SMEM),
                  pl.BlockSpec(memory_space=pl.ANY),
                  pl.BlockSpec((1, D_MODEL, 2 * D_FF), lambda i, te: (te[i], 0, 0)),
                  pl.BlockSpec((2 * LANES, 2 * LANES), lambda i, te: (0, 0)),
                  pl.BlockSpec((1, 1, 2 * D_FF), lambda i, te: (te[i], 0, 0)),
                  pl.BlockSpec((1, D_FF, D_MODEL), lambda i, te: (te[i], 0, 0)),
                  pl.BlockSpec((1, 1, D_MODEL), lambda i, te: (te[i], 0, 0))],
        out_specs=pl.BlockSpec((tm * ROW_SPLIT, LANES), lambda i, te: (i, 0)),
        scratch_shapes=[pltpu.VMEM((tm * ROW_SPLIT, LANES), F32),
                        pltpu.VMEM((tm * ROW_SPLIT, LANES), F32),
                        pltpu.VMEM((D_MODEL, 2 * D_FF), BF16),
                        pltpu.VMEM((D_FF, D_MODEL), BF16),
                        pltpu.SemaphoreType.DMA((2,))],
    )
    return pl.pallas_call(
        _expert_kernel,
        grid_spec=grid_spec,
        out_shape=jax.ShapeDtypeStruct((ntiles * tm * ROW_SPLIT, LANES), F32),
        compiler_params=_cparams(("arbitrary",)),
    )(tile_expert, src3, src3, h2, w1, jnp.asarray(perm, BF16), b1, w2, b2)


def _combine_kernel(pos_ref, nxt_ref, y_hbm, wt_ref, x_ref, pg_ref, g_ref, o_ref, ya, yb, sem):
    i = pl.program_id(0) * pl.num_programs(1) + pl.program_id(1)
    last = pl.num_programs(0) * pl.num_programs(1) - 1
    tm = x_ref.shape[0]

    def start(idx_ref, buf, s):
        for k in range(TOP_K):
            _start_rows(idx_ref, tm, y_hbm, buf.at[k], s, idx_off=k * tm)

    def wait(buf, s):
        for k in range(TOP_K):
            _wait_rows(tm, y_hbm, buf.at[k], s)

    @pl.when(i == 0)
    def _():
        start(pos_ref, ya, sem.at[0])

    def step(cur, cur_sem, nxt, nxt_sem):
        wait(cur, cur_sem)
        start(nxt_ref, nxt, nxt_sem)
        wt = wt_ref[...]
        y = wt[:, 0:1] * _tile_rows(cur.at[0], tm)
        for k in range(1, TOP_K):
            y = y + wt[:, k:k + 1] * _tile_rows(cur.at[k], tm)
        ms = jnp.mean(y * y, axis=-1, keepdims=True)
        o_ref[...] = x_ref[...] + g_ref[0] * (y * lax.rsqrt(ms + EPS) * pg_ref[...])

        @pl.when(i == last)
        def _():
            wait(nxt, nxt_sem)

    @pl.when(lax.rem(i, 2) == 0)
    def _():
        step(ya, sem.at[0], yb, sem.at[1])

    @pl.when(lax.rem(i, 2) == 1)
    def _():
        step(yb, sem.at[1], ya, sem.at[0])


def _combine_call(dest, y_sorted, wts, x1, post_g, g2, batch, seq):
    t = x1.shape[0]
    tm = min(CMB_TILE, seq)
    nt = seq // tm
    ntiles = t // tm
    dest_t = dest.reshape(ntiles, tm, TOP_K).transpose(0, 2, 1).reshape(ntiles, 1, TOP_K * tm)
    row = lambda b, i: (b * nt + i, 0)
    return pl.pallas_call(
        _combine_kernel,
        grid=(batch, nt),
        in_specs=[pl.BlockSpec((1, 1, TOP_K * tm), lambda b, i: (b * nt + i, 0, 0),
                               memory_space=pltpu.SMEM),
                  pl.BlockSpec((1, 1, TOP_K * tm),
                               lambda b, i: (jnp.minimum(b * nt + i + 1, ntiles - 1), 0, 0),
                               memory_space=pltpu.SMEM),
                  pl.BlockSpec(memory_space=pl.ANY),
                  pl.BlockSpec((tm, LANES), row),
                  pl.BlockSpec((tm, D_MODEL), row),
                  pl.BlockSpec((1, D_MODEL), lambda b, i: (0, 0)),
                  pl.BlockSpec((1, 1, D_MODEL), lambda b, i: (b, 0, 0))],
        out_specs=pl.BlockSpec((tm, D_MODEL), row),
        out_shape=jax.ShapeDtypeStruct((t, D_MODEL), F32),
        scratch_shapes=[pltpu.VMEM((TOP_K, tm * ROW_SPLIT, LANES), F32),
                        pltpu.VMEM((TOP_K, tm * ROW_SPLIT, LANES), F32),
                        pltpu.SemaphoreType.DMA((2,))],
        compiler_params=_cparams(("arbitrary", "arbitrary")),
    )(dest_t, dest_t, y_sorted, wts, x1, post_g, g2)


def _dispatch_plan(idx, rank, counts, t):
    tm = EXP_TILE
    npairs = t * TOP_K
    ntiles = npairs // tm + N_EXPERTS
    e_flat = idx.reshape(-1)
    pair = jnp.arange(npairs, dtype=jnp.int32)
    _, order = lax.sort((e_flat, pair), num_keys=1, is_stable=True)
    cnt_excl = jnp.cumsum(counts) - counts
    tiles_per = (counts + tm - 1) // tm
    tile_end = jnp.cumsum(tiles_per)
    tile_start = tile_end - tiles_per
    tile_ids = jnp.arange(ntiles, dtype=jnp.int32)
    tile_expert = jnp.minimum(
        jnp.sum((tile_ids[:, None] >= tile_end[None, :]).astype(jnp.int32), axis=1),
        N_EXPERTS - 1).astype(jnp.int32)
    rows = jnp.arange(ntiles * tm, dtype=jnp.int32)
    row_e = jnp.repeat(tile_expert, tm)
    off = rows - tile_start[row_e] * tm
    valid = (off < counts[row_e]) & (rows < tile_end[-1] * tm)
    sidx = jnp.clip(cnt_excl[row_e] + off, 0, npairs - 1)
    src_rows = jnp.where(valid, order[sidx] // TOP_K, 0).astype(jnp.int32)
    dest = (tile_start[idx] * tm + rank).astype(jnp.int32)
    return tile_expert, src_rows, dest


def _rope_tables():
    lane = np.arange(GROUP)
    d = lane % DIFF_D
    rd = DIFF_D // 4
    half = rd // 2
    inv = np.where(d < rd, ROPE_THETA ** (-(d % half).astype(np.float32) / half), 0.0)
    ma = np.where(d < half, -1.0, 0.0)
    mb = np.where((d >= half) & (d < rd), 1.0, 0.0)
    f = lambda a: jnp.asarray(a.reshape(1, GROUP), F32)
    return f(inv), f(ma), f(mb)


def kernel(x, c, positions, ada_w, ada_b, pre_norm_g, post_norm_g, w_in, w_out, fox_fb, hg_lower, hg_norm_g, diff_lam_q1, diff_lam_k1, diff_lam_q2, diff_lam_k2, diff_subln_g, s5_a_re, s5_a_im, s5_log_step, s5_b_re, s5_b_im, s5_c_re, s5_c_im, s5_d, s5_glu_w, router_w, router_b, exp_w1, exp_b1, exp_w2, exp_b2):
    batch, seq, d = x.shape
    depth = ada_w.shape[0]
    t = batch * seq
    assert d == D_MODEL and seq % ROW_TILE == 0

    lb_all = jnp.cumsum(jax.nn.softmax(hg_lower.astype(F32), axis=0), axis=0)
    lb_all = lb_all - lb_all[0:1]
    mod = _ada_call(c, ada_w, ada_b)
    inv_lane, mask_a, mask_b = _rope_tables()
    pos = positions.reshape(t, 1).astype(jnp.int32)
    xf = x.reshape(t, d)

    fcols = 3 * GROUP
    for l in range(depth):
        m6 = mod[l].reshape(batch, 6, 1, d)
        sh1, sc1, g1, sh2, sc2, g2 = [m6[:, j] for j in range(6)]
        wl = w_in[l]
        w_main = jnp.concatenate([wl[:, 0:fcols], wl[:, fcols + FOX_HEADS:]], axis=1).astype(BF16)
        w_f = jnp.pad(wl[:, fcols:fcols + FOX_HEADS], ((0, 0), (0, LANES - FOX_HEADS))).astype(BF16)
        fb = jnp.pad(fox_fb[l], (0, LANES - FOX_HEADS)).reshape(1, LANES)
        fox, frow, hg, diff, s5u = _inproj_call(
            xf, pre_norm_g[l, 0].reshape(1, d), sc1, sh1, w_main, w_f, fb, pos,
            inv_lane, mask_a, mask_b, batch, seq)

        y_fox = _fox_call(fox, frow, batch, seq)

        lam_init = 0.8 - 0.6 * math.exp(-0.3 * l)
        lam = (jnp.exp(jnp.sum(diff_lam_q1[l].astype(F32) * diff_lam_k1[l].astype(F32)))
               - jnp.exp(jnp.sum(diff_lam_q2[l].astype(F32) * diff_lam_k2[l].astype(F32))) + lam_init)
        y_diff = _diff_call(diff, lam.reshape(1), diff_subln_g[l].reshape(1, DIFF_DV),
                            1.0 - lam_init, batch, seq)

        y_hg = _hgrn2_call(hg, lb_all[l].reshape(1, -1), hg_norm_g[l].reshape(1, HG_DV), batch, seq)

        bbar, ar, ai, cmat = _s5_params(s5_a_re[l], s5_a_im[l], s5_log_step[l], s5_b_re[l],
                                        s5_b_im[l], s5_c_re[l], s5_c_im[l])
        u_sb = s5u.reshape(batch, seq, GROUP).transpose(1, 0, 2).reshape(t, GROUP)
        y_s5_sb = _s5_call(u_sb, bbar, ar, ai, cmat, s5_d[l].reshape(1, GROUP),
                           s5_glu_w[l].astype(BF16), batch, seq)
        y_s5 = y_s5_sb.reshape(seq, batch, GROUP).transpose(1, 0, 2).reshape(t, GROUP)

        rw = jnp.pad(router_w[l], ((0, 0), (0, LANES - N_EXPERTS)))
        rb = jnp.pad(router_b[l], (0, LANES - N_EXPERTS), constant_values=NEG_BIG).reshape(1, LANES)
        x1, h2, idx, wts, cnt = _outproj_call(
            y_fox, y_hg, y_diff, y_s5, w_out[l].astype(BF16), xf,
            post_norm_g[l, 0].reshape(1, d), g1, pre_norm_g[l, 1].reshape(1, d), sc2, sh2,
            rw, rb, batch, seq)

        tile_expert, src_rows, dest = _dispatch_plan(
            idx[:, 0:TOP_K], idx[:, TOP_K:2 * TOP_K], cnt[0, 0:N_EXPERTS].astype(jnp.int32), t)
        b1 = exp_b1[l]
        b1p = jnp.concatenate([b1[:, 0::2], b1[:, 1::2]], axis=-1).reshape(N_EXPERTS, 1, 2 * D_FF)
        y_sorted = _expert_call(tile_expert, src_rows, h2, exp_w1[l], b1p, exp_w2[l],
                                exp_b2[l].reshape(N_EXPERTS, 1, D_MODEL))
        xf = _combine_call(dest, y_sorted, wts, x1, post_norm_g[l, 1].reshape(1, d), g2, batch, seq)

    return xf.reshape(batch, seq, d)
```

```python
import functools
import math

import numpy as np
import jax
import jax.numpy as jnp
from jax import lax
from jax.experimental import pallas as pl
from jax.experimental.pallas import tpu as pltpu

F32 = jnp.float32
BF16 = jnp.bfloat16

D_MODEL = 1024
GROUP = 256
FOX_HEADS = 4
FOX_HD = 64
HG_HEADS = 4
HG_DK = 128
HG_DV = 64
DIFF_HEADS = 4
DIFF_DV = 64
DIFF_D = 32
S5_CH = 16
S5_GROUPS = 16
S5_N = 64
ROPE_THETA = 500000.0
N_EXPERTS = 32
TOP_K = 4
D_FF = 1024
SWIGLU_LIMIT = 7.0
SWIGLU_ALPHA = 1.702
EPS = 1e-6
NEG_BIG = -1e30

LANES = 128
VMEM_LIMIT = 56 * 1024 * 1024

LOG2E = 1.4426950408889634

ROW_TILE = 512
ATT_T = ROW_TILE
HG_CHUNK = 64
S5_STEPS = 128
EXP_TILE = 512
CMB_TILE = 256

ROW_SPLIT = D_MODEL // LANES

HG_COLS = 2 * HG_HEADS * HG_DK + 2 * GROUP
MAIN_COLS = 3 * GROUP + HG_COLS + 3 * GROUP + GROUP


def _cparams(sem):
    return pltpu.CompilerParams(dimension_semantics=sem, vmem_limit_bytes=VMEM_LIMIT)


def _sigmoid(x):
    return 1.0 / (1.0 + jnp.exp(-x))


def _shr(x, pow2):
    return lax.shift_right_logical(x, int(math.log2(pow2)))


def _dot(a, b):
    return jnp.dot(a, b, preferred_element_type=F32)


def _dot_nt(a, b):
    return lax.dot_general(a, b, (((1,), (1,)), ((), ())), preferred_element_type=F32)


def _dot_tn(a, b):
    return lax.dot_general(a, b, (((0,), (0,)), ((), ())), preferred_element_type=F32)


def _split_dot(m_bf16, x, terms=3):
    out = None
    for _ in range(terms):
        piece = x.astype(BF16)
        x = x - piece.astype(F32)
        d = _dot(m_bf16, piece)
        out = d if out is None else out + d
    return out


def _ada_kernel(c_ref, w_ref, b_ref, o_ref):
    c = c_ref[...]
    cs = c * _sigmoid(c)
    o_ref[0] = jnp.dot(cs, w_ref[0], preferred_element_type=F32,
                       precision=lax.Precision.HIGHEST) + b_ref[0]


def _ada_call(c, ada_w, ada_b):
    depth, d, n = ada_w.shape
    b = c.shape[0]
    tn = 1536
    return pl.pallas_call(
        _ada_kernel,
        grid=(depth, n // tn),
        in_specs=[pl.BlockSpec((b, d), lambda l, j: (0, 0)),
                  pl.BlockSpec((1, d, tn), lambda l, j: (l, 0, j)),
                  pl.BlockSpec((1, 1, tn), lambda l, j: (l, 0, j))],
        out_specs=pl.BlockSpec((1, b, tn), lambda l, j: (l, 0, j)),
        out_shape=jax.ShapeDtypeStruct((depth, b, n), F32),
        compiler_params=_cparams(("arbitrary", "arbitrary")),
    )(c, ada_w, ada_b.reshape(depth, 1, n))


_FOX_COLS = 3 * GROUP


def _win_prep_kernel(w_ref, main_ref, f_ref):
    rows = w_ref.shape[1]
    lane = lax.broadcasted_iota(jnp.int32, (rows, LANES), 1)
    main_ref[0, :, 0:_FOX_COLS] = w_ref[0, :, 0:_FOX_COLS].astype(BF16)
    first = w_ref[0, :, _FOX_COLS:_FOX_COLS + LANES]
    f_ref[0] = jnp.where(lane < FOX_HEADS, first, 0.0).astype(BF16)
    cur = pltpu.roll(first, LANES - FOX_HEADS, axis=1)
    for m in range((MAIN_COLS - _FOX_COLS) // LANES):
        c0 = _FOX_COLS + m * LANES
        nxt = pltpu.roll(w_ref[0, :, c0 + LANES:c0 + 2 * LANES], LANES - FOX_HEADS, axis=1)
        main_ref[0, :, c0:c0 + LANES] = jnp.where(lane < LANES - FOX_HEADS, cur, nxt).astype(BF16)
        cur = nxt


def _win_prep_call(w_in):
    depth, d, cols = w_in.shape
    padded = MAIN_COLS + LANES
    w_pad = jnp.pad(w_in, ((0, 0), (0, 0), (0, padded - cols)))
    rows = 256
    return pl.pallas_call(
        _win_prep_kernel,
        grid=(depth, d // rows),
        in_specs=[pl.BlockSpec((1, rows, padded), lambda l, r: (l, r, 0))],
        out_specs=[pl.BlockSpec((1, rows, MAIN_COLS), lambda l, r: (l, r, 0)),
                   pl.BlockSpec((1, rows, LANES), lambda l, r: (l, r, 0))],
        out_shape=[jax.ShapeDtypeStruct((depth, d, MAIN_COLS), BF16),
                   jax.ShapeDtypeStruct((depth, d, LANES), BF16)],
        compiler_params=_cparams(("arbitrary", "arbitrary")),
    )(w_pad)


def _inproj_kernel(x_ref, g_ref, sc_ref, sh_ref, w_ref, wf_ref, fb_ref, pos_ref, inv_ref,
                   ma_ref, mb_ref,
                   fox_ref, frow_ref, hg_ref, diff_ref, s5_ref, carry_ref):
    i = pl.program_id(1)
    tm = x_ref.shape[0]

    @pl.when(i == 0)
    def _():
        carry_ref[...] = jnp.zeros_like(carry_ref)

    x = x_ref[...]
    ms = jnp.mean(x * x, axis=-1, keepdims=True)
    h = (x * lax.rsqrt(ms + EPS) * g_ref[...]) * (1.0 + sc_ref[0]) + sh_ref[0]
    hb = h.astype(BF16)

    o = 0
    pf = _dot(hb, w_ref[0, :, o:o + 3 * GROUP])
    fox_ref[:, 0:GROUP] = (pf[:, 0:GROUP] * (FOX_HD ** -0.5 * LOG2E)).astype(BF16)
    fox_ref[:, GROUP:3 * GROUP] = pf[:, GROUP:3 * GROUP].astype(BF16)
    o += 3 * GROUP

    ff = _dot(hb, wf_ref[0]) + fb_ref[...]
    logf = jnp.minimum(ff, 0.0) - jnp.log(1.0 + jnp.exp(-jnp.abs(ff)))
    r = lax.broadcasted_iota(jnp.int32, (tm, tm), 0)
    cidx = lax.broadcasted_iota(jnp.int32, (tm, tm), 1)
    tri = jnp.where(cidx <= r, 1.0, 0.0).astype(BF16)
    cum = _split_dot(tri, logf) + carry_ref[...]
    carry_ref[...] = cum[tm - 1:tm, :]
    cum_t = jnp.transpose(cum * LOG2E)
    frow_ref[0, 0] = cum_t[0:8, :]

    hg_ref[...] = _dot(hb, w_ref[0, :, o:o + HG_COLS]).astype(BF16)
    o += HG_COLS

    pd = _dot(hb, w_ref[0, :, o:o + 3 * GROUP])
    o += 3 * GROUP
    pos = pos_ref[...].astype(F32)
    ang = pos * inv_ref[...]
    cos = jnp.cos(ang)
    sin = jnp.sin(ang)
    ma = ma_ref[...]
    mb = mb_ref[...]

    def rope(xx):
        outs = []
        for j in range(GROUP // LANES):
            sl = slice(j * LANES, (j + 1) * LANES)
            xs = xx[:, sl]
            partner = (pltpu.roll(xs, LANES - 4, axis=1) * ma[:, sl]
                       + pltpu.roll(xs, 4, axis=1) * mb[:, sl])
            outs.append(xs * cos[:, sl] + partner * sin[:, sl])
        return jnp.concatenate(outs, axis=1)

    diff_ref[:, 0:GROUP] = (rope(pd[:, 0:GROUP]) * (DIFF_D ** -0.5 * LOG2E)).astype(BF16)
    diff_ref[:, GROUP:2 * GROUP] = rope(pd[:, GROUP:2 * GROUP]).astype(BF16)
    diff_ref[:, 2 * GROUP:3 * GROUP] = pd[:, 2 * GROUP:3 * GROUP].astype(BF16)

    s5_ref[...] = _dot(hb, w_ref[0, :, o:o + GROUP]).astype(BF16)


def _inproj_call(x2, g_pre, sc, sh, w_main, w_f, fb, pos, inv_lane, mask_a, mask_b, batch, seq,
                 layer):
    t = x2.shape[0]
    tm = ROW_TILE
    nt = seq // tm
    row = lambda b, i: (b * nt + i, 0)
    const2 = lambda b, i: (0, 0)
    per_b = lambda b, i: (b, 0, 0)
    outs = pl.pallas_call(
        _inproj_kernel,
        grid=(batch, nt),
        in_specs=[pl.BlockSpec((tm, D_MODEL), row),
                  pl.BlockSpec((1, D_MODEL), const2),
                  pl.BlockSpec((1, 1, D_MODEL), per_b),
                  pl.BlockSpec((1, 1, D_MODEL), per_b),
                  pl.BlockSpec((1, D_MODEL, MAIN_COLS), lambda b, i: (layer, 0, 0)),
                  pl.BlockSpec((1, D_MODEL, LANES), lambda b, i: (layer, 0, 0)),
                  pl.BlockSpec((1, LANES), const2),
                  pl.BlockSpec((tm, 1), row),
                  pl.BlockSpec((1, GROUP), const2),
                  pl.BlockSpec((1, GROUP), const2),
                  pl.BlockSpec((1, GROUP), const2)],
        out_specs=[pl.BlockSpec((tm, 3 * GROUP), row),
                   pl.BlockSpec((1, 1, 8, tm), lambda b, i: (b, i, 0, 0)),
                   pl.BlockSpec((tm, HG_COLS), row),
                   pl.BlockSpec((tm, 3 * GROUP), row),
                   pl.BlockSpec((tm, GROUP), row)],
        out_shape=[jax.ShapeDtypeStruct((t, 3 * GROUP), BF16),
                   jax.ShapeDtypeStruct((batch, nt, 8, tm), F32),
                   jax.ShapeDtypeStruct((t, HG_COLS), BF16),
                   jax.ShapeDtypeStruct((t, 3 * GROUP), BF16),
                   jax.ShapeDtypeStruct((t, GROUP), BF16)],
        scratch_shapes=[pltpu.VMEM((1, LANES), F32)],
        compiler_params=_cparams(("arbitrary", "arbitrary")),
    )(x2, g_pre, sc, sh, w_main, w_f, fb, pos, inv_lane, mask_a, mask_b)
    return outs


_HEAD_W = 64
_HEADS = 4


def _slot(x64):
    return jnp.concatenate([x64, jnp.zeros_like(x64)], axis=1)


def _pad_kv(k_ref, v_ref, k_s, v_s):
    seq = k_ref.shape[0]
    rows = ATT_T
    lane = lax.broadcasted_iota(jnp.int32, (rows, LANES), 1)

    def body(c, _):
        r0 = pl.multiple_of(c * rows, rows)
        for h in range(_HEADS):
            k_s[pl.ds(r0, rows), h * LANES:(h + 1) * LANES] = _slot(
                k_ref[pl.ds(r0, rows), h * _HEAD_W:(h + 1) * _HEAD_W])
            v = _slot(v_ref[pl.ds(r0, rows), h * _HEAD_W:(h + 1) * _HEAD_W])
            v_s[pl.ds(r0, rows), h * LANES:(h + 1) * LANES] = jnp.where(
                lane == _HEAD_W, jnp.ones_like(v), v)
        return 0

    lax.fori_loop(0, seq // rows, body, 0)


def _attend(qs, slots, scratch, qi, key_bias):
    k_s, v_s, m_s, acc_s, q_s = scratch
    n = len(qs)
    tq = qs[0].shape[0]
    tk = ATT_T

    def update(q, m, acc, start, nkeys, kbias, keep, c0):
        s = _dot_nt(q, k_s[pl.ds(start, nkeys), c0:c0 + LANES])
        if kbias is not None:
            s = s - kbias
        if keep is not None:
            s = jnp.where(keep, s, NEG_BIG)
        m_new = jnp.maximum(m, jnp.max(s, axis=-1, keepdims=True))
        alpha = jnp.exp2(m - m_new)
        p = jnp.exp2(s - m_new).astype(BF16)
        return m_new, alpha * acc + _dot(p, v_s[pl.ds(start, nkeys), c0:c0 + LANES])

    def block(kb, carry):
        ms, accs = carry
        start = pl.multiple_of(kb * tk, tk)
        out = [update(qs[j], ms[j], accs[j], start, tk, key_bias(j, kb), None, slots[j] * LANES)
               for j in range(n)]
        return tuple(o[0] for o in out), tuple(o[1] for o in out)

    def diagonal(carry):
        ms, accs = carry
        start = pl.multiple_of(qi * tk, tk)
        half = tq // 2
        keep_top = (lax.broadcasted_iota(jnp.int32, (half, half), 1)
                    <= lax.broadcasted_iota(jnp.int32, (half, half), 0))
        keep_bot = (lax.broadcasted_iota(jnp.int32, (half, tk), 1)
                    <= lax.broadcasted_iota(jnp.int32, (half, tk), 0) + half)
        new_acc = []
        for j in range(n):
            c0 = slots[j] * LANES
            kbias = key_bias(j, qi)
            m_s[j] = ms[j]
            acc_s[j] = accs[j]
            q_s[j] = qs[j]
            _, top = update(q_s[j, 0:half], m_s[j, 0:half], acc_s[j, 0:half], start, half,
                            None if kbias is None else kbias[:, 0:half], keep_top, c0)
            _, bot = update(q_s[j, half:tq], m_s[j, half:tq], acc_s[j, half:tq], start, tk,
                            kbias, keep_bot, c0)
            acc_s[j, 0:half] = top
            acc_s[j, half:tq] = bot
            new_acc.append(acc_s[j])
        return new_acc

    init = (tuple(jnp.full((tq, 1), NEG_BIG, F32) for _ in range(n)),
            tuple(jnp.zeros((tq, LANES), F32) for _ in range(n)))
    return diagonal(lax.fori_loop(0, qi, block, init))


def _fox_kernel(q_ref, k_ref, v_ref, frow_ref, o_ref, k_s, v_s, m_s, acc_s, q_s):
    qi = pl.program_id(1)

    @pl.when(qi == 0)
    def _():
        _pad_kv(k_ref, v_ref, k_s, v_s)

    qs = [_slot(q_ref[:, h * _HEAD_W:(h + 1) * _HEAD_W]) for h in range(_HEADS)]
    accs = _attend(qs, list(range(_HEADS)), (k_s, v_s, m_s, acc_s, q_s), qi,
                   lambda j, kb: frow_ref[0, kb, j:j + 1, :])
    outs = [a[:, 0:_HEAD_W] / a[:, _HEAD_W:_HEAD_W + 1] for a in accs]
    o_ref[...] = jnp.concatenate(outs, axis=1).astype(BF16)


def _att_specs(seq):
    nq = seq // ATT_T
    return ([pl.BlockSpec((ATT_T, GROUP), lambda b, i: (b * nq + i, 0)),
             pl.BlockSpec((seq, GROUP), lambda b, i: (b, 1)),
             pl.BlockSpec((seq, GROUP), lambda b, i: (b, 2))],
            pl.BlockSpec((ATT_T, GROUP), lambda b, i: (b * nq + i, 0)),
            [pltpu.VMEM((seq, _HEADS * LANES), BF16), pltpu.VMEM((seq, _HEADS * LANES), BF16),
             pltpu.VMEM((_HEADS, ATT_T, 1), F32), pltpu.VMEM((_HEADS, ATT_T, LANES), F32),
             pltpu.VMEM((_HEADS, ATT_T, LANES), BF16)])


def _fox_call(fox, frow, batch, seq):
    t = fox.shape[0]
    nq = seq // ATT_T
    qkv_specs, out_spec, scratch = _att_specs(seq)
    return pl.pallas_call(
        _fox_kernel,
        grid=(batch, nq),
        in_specs=qkv_specs + [pl.BlockSpec((1, nq, 8, ATT_T), lambda b, i: (b, 0, 0, 0))],
        out_specs=out_spec,
        out_shape=jax.ShapeDtypeStruct((t, GROUP), BF16),
        scratch_shapes=scratch,
        compiler_params=_cparams(("arbitrary", "arbitrary")),
    )(fox, fox, fox, frow)


def _diff_kernel(lam_ref, q_ref, k_ref, v_ref, g_ref, o_ref, k_s, v_s, m_s, acc_s, q_s, *,
                 out_scale):
    qi = pl.program_id(1)

    @pl.when(qi == 0)
    def _():
        _pad_kv(k_ref, v_ref, k_s, v_s)

    lam = lam_ref[0]
    lane = lax.broadcasted_iota(jnp.int32, (ATT_T, LANES), 1)
    outs = []
    for pair in range(_HEADS // 2):
        qs, slots = [], []
        for h in (2 * pair, 2 * pair + 1):
            q = _slot(q_ref[:, h * _HEAD_W:(h + 1) * _HEAD_W])
            qs += [jnp.where(lane < DIFF_D, q, jnp.zeros_like(q)),
                   jnp.where(lane >= DIFF_D, q, jnp.zeros_like(q))]
            slots += [h, h]
        accs = _attend(qs, slots, (k_s, v_s, m_s, acc_s, q_s), qi, lambda j, kb: None)
        for j in (0, 2):
            y = (accs[j][:, 0:_HEAD_W] / accs[j][:, _HEAD_W:_HEAD_W + 1]
                 - lam * (accs[j + 1][:, 0:_HEAD_W] / accs[j + 1][:, _HEAD_W:_HEAD_W + 1]))
            ms = jnp.mean(y * y, axis=-1, keepdims=True)
            outs.append(y * lax.rsqrt(ms + EPS) * g_ref[...] * out_scale)
    o_ref[...] = jnp.concatenate(outs, axis=1).astype(BF16)


def _diff_call(diff, lam, subln_g, out_scale, batch, seq):
    t = diff.shape[0]
    nq = seq // ATT_T
    qkv_specs, out_spec, scratch = _att_specs(seq)
    return pl.pallas_call(
        functools.partial(_diff_kernel, out_scale=out_scale),
        grid=(batch, nq),
        in_specs=([pl.BlockSpec(memory_space=pltpu.SMEM)] + qkv_specs
                  + [pl.BlockSpec((1, DIFF_DV), lambda b, i: (0, 0))]),
        out_specs=out_spec,
        out_shape=jax.ShapeDtypeStruct((t, GROUP), BF16),
        scratch_shapes=scratch,
        compiler_params=_cparams(("arbitrary", "arbitrary")),
    )(lam, diff, diff, diff, subln_g)


_HG_LEVELS = (32, 16, 8)


def _hg_ref_rows(G, level):
    C = G.shape[0]
    rows = []
    for tile in range(C // 8):
        if level is None:
            r = tile * 8 + 3
        else:
            grp = (tile * 8) // (2 * level)
            r = grp * 2 * level + level - 1
        rows.append(jnp.broadcast_to(G[r:r + 1, :], (8, G.shape[1])))
    return jnp.concatenate(rows, axis=0)


def _hgrn2_kernel(p_ref, lb_ref, ng_ref, o_ref, state_ref):
    i = pl.program_id(1)
    tm = p_ref.shape[0]
    C = HG_CHUNK
    KW = HG_HEADS * HG_DK

    @pl.when(i == 0)
    def _():
        state_ref[...] = jnp.zeros_like(state_ref)

    lb = lb_ref[...]
    rr = lax.broadcasted_iota(jnp.int32, (C, C), 0)
    cc = lax.broadcasted_iota(jnp.int32, (C, C), 1)
    tri = jnp.where(cc <= rr, 1.0, 0.0).astype(BF16)
    level_masks = []
    for m in _HG_LEVELS:
        same_grp = _shr(rr, 2 * m) == _shr(cc, 2 * m)
        level_masks.append(same_grp & ((rr & (2 * m - 1)) >= m) & ((cc & (2 * m - 1)) < m))
    diag_mask = (_shr(rr, 8) == _shr(cc, 8)) & (cc <= rr)
    row_idx = lax.broadcasted_iota(jnp.int32, (C, HG_DK), 0)

    def chunk(base, states):
        hq = p_ref[pl.ds(base, C), 0:KW].astype(F32)
        q_all = hq * _sigmoid(hq)
        f = lb + (1.0 - lb) * _sigmoid(p_ref[pl.ds(base, C), KW:2 * KW].astype(F32))
        k_all = 1.0 - f
        g_all = _split_dot(tri, jnp.log(f), terms=2)
        new_states = []
        for h in range(HG_HEADS):
            ks = slice(h * HG_DK, (h + 1) * HG_DK)
            q = q_all[:, ks]
            k = k_all[:, ks]
            G = g_all[:, ks]
            v = p_ref[pl.ds(base, C), 2 * KW + h * HG_DV:2 * KW + (h + 1) * HG_DV]
            st = states[h]
            gl = G[C - 1:C, :]
            o = _dot_nt((q * jnp.exp(G)).astype(BF16), st.astype(BF16))
            a = jnp.zeros((C, C), F32)
            for m, mask in zip(_HG_LEVELS, level_masks):
                ref = _hg_ref_rows(G, m)
                upper = (row_idx & (2 * m - 1)) >= m
                e = jnp.exp(jnp.where(upper, G - ref, ref - G))
                am = _dot_nt((q * e).astype(BF16), (k * e).astype(BF16))
                a = a + jnp.where(mask, am, 0.0)
            ref = _hg_ref_rows(G, None)
            d = jnp.clip(G - ref, -80.0, 80.0)
            am = _dot_nt((q * jnp.exp(d)).astype(BF16), (k * jnp.exp(-d)).astype(BF16))
            a = a + jnp.where(diag_mask, am, 0.0)
            o = o + _dot(a.astype(BF16), v)
            kd = (k * jnp.exp(gl - G)).astype(BF16)
            new_states.append(jnp.exp(gl) * st + _dot_tn(v, kd))
            ms = jnp.mean(o * o, axis=-1, keepdims=True)
            on = o * lax.rsqrt(ms + EPS) * ng_ref[...]
            gate = p_ref[pl.ds(base, C), 2 * KW + GROUP + h * HG_DV:
                         2 * KW + GROUP + (h + 1) * HG_DV].astype(F32)
            o_ref[pl.ds(base, C), h * HG_DV:(h + 1) * HG_DV] = (
                on * (gate * _sigmoid(gate))).astype(BF16)
        return tuple(new_states)

    def pair(ci, states):
        first = pl.multiple_of(ci * 2 * C, 2 * C)
        second = pl.multiple_of(ci * 2 * C + C, C)
        return chunk(second, chunk(first, states))

    states = lax.fori_loop(0, tm // (2 * C), pair,
                           tuple(state_ref[h] for h in range(HG_HEADS)))
    for h in range(HG_HEADS):
        state_ref[h] = states[h]


def _hgrn2_call(hg, lb, norm_g, batch, seq):
    t = hg.shape[0]
    tm = min(ROW_TILE, seq)
    nt = seq // tm
    kw = HG_HEADS * HG_DK
    return pl.pallas_call(
        _hgrn2_kernel,
        grid=(batch, nt),
        in_specs=[pl.BlockSpec((tm, HG_COLS), lambda b, i: (b * nt + i, 0)),
                  pl.BlockSpec((1, kw), lambda b, i: (0, 0)),
                  pl.BlockSpec((1, HG_DV), lambda b, i: (0, 0))],
        out_specs=pl.BlockSpec((tm, GROUP), lambda b, i: (b * nt + i, 0)),
        out_shape=jax.ShapeDtypeStruct((t, GROUP), BF16),
        scratch_shapes=[pltpu.VMEM((HG_HEADS, HG_DV, HG_DK), F32)],
        compiler_params=_cparams(("arbitrary", "arbitrary")),
    )(hg, lb, norm_g)


def _s5_kernel(u_ref, bb_ref, ar_ref, ai_ref, cm_ref, d_ref, glu_ref, o_ref, bu_s, xr_s, xi_s,
               *, batch):
    i = pl.program_id(0)
    nst = S5_GROUPS * S5_N
    steps = u_ref.shape[0] // batch

    @pl.when(i == 0)
    def _():
        xr_s[...] = jnp.zeros_like(xr_s)
        xi_s[...] = jnp.zeros_like(xi_s)

    u = u_ref[...]
    bu_s[...] = _dot(u, bb_ref[...])
    ar = jnp.broadcast_to(ar_ref[...], (batch, nst))
    ai = jnp.broadcast_to(ai_ref[...], (batch, nst))

    def step(t, carry):
        xr, xi = carry
        r0 = pl.multiple_of(t * batch, batch)
        nr = ar * xr - ai * xi + bu_s[pl.ds(r0, batch), 0:nst]
        ni = ar * xi + ai * xr + bu_s[pl.ds(r0, batch), nst:2 * nst]
        bu_s[pl.ds(r0, batch), 0:nst] = nr
        bu_s[pl.ds(r0, batch), nst:2 * nst] = ni
        return nr, ni

    xr, xi = lax.fori_loop(0, steps, step, (xr_s[...], xi_s[...]))
    xr_s[...] = xr
    xi_s[...] = xi
    y = _dot(bu_s[...].astype(BF16), cm_ref[...]) + d_ref[...] * u.astype(F32)
    gel = 0.5 * y * (1.0 + jnp.tanh(0.7978845608028654 * (y + 0.044715 * (y * y * y))))
    z = _dot(gel.astype(BF16), glu_ref[...])
    o_ref[...] = (z[:, 0:GROUP] * _sigmoid(z[:, GROUP:2 * GROUP])).astype(BF16)


def _s5_call(u_sb, bbar, ar, ai, cmat, dskip, glu_w, batch, seq):
    rows = u_sb.shape[0]
    steps = min(S5_STEPS, seq)
    tr = steps * batch
    nst = S5_GROUPS * S5_N
    const = lambda i: (0, 0)
    return pl.pallas_call(
        functools.partial(_s5_kernel, batch=batch),
        grid=(seq // steps,),
        in_specs=[pl.BlockSpec((tr, GROUP), lambda i: (i, 0)),
                  pl.BlockSpec((GROUP, 2 * nst), const),
                  pl.BlockSpec((1, nst), const),
                  pl.BlockSpec((1, nst), const),
                  pl.BlockSpec((2 * nst, GROUP), const),
                  pl.BlockSpec((1, GROUP), const),
                  pl.BlockSpec((GROUP, 2 * GROUP), const)],
        out_specs=pl.BlockSpec((tr, GROUP), lambda i: (i, 0)),
        out_shape=jax.ShapeDtypeStruct((rows, GROUP), BF16),
        scratch_shapes=[pltpu.VMEM((tr, 2 * nst), F32),
                        pltpu.VMEM((batch, nst), F32), pltpu.VMEM((batch, nst), F32)],
        compiler_params=_cparams(("arbitrary",)),
    )(u_sb, bbar, ar, ai, cmat, dskip, glu_w)


def _s5_params(a_re, a_im, log_step, b_re, b_im, c_re, c_im):
    lr = jnp.minimum(a_re.astype(F32), -1e-4)
    li = a_im.astype(F32)
    dt = jnp.exp(log_step.astype(F32))[:, None]
    mag = jnp.exp(lr * dt)
    ar = mag * jnp.cos(li * dt)
    ai = mag * jnp.sin(li * dt)
    den = lr * lr + li * li
    zr = ((ar - 1.0) * lr + ai * li) / den
    zi = (ai * lr - (ar - 1.0) * li) / den
    br = b_re.astype(F32)
    bi = b_im.astype(F32)
    bbr = zr[..., None] * br - zi[..., None] * bi
    bbi = zr[..., None] * bi + zi[..., None] * br
    eye = jnp.eye(S5_GROUPS, dtype=F32)
    def bd_in(m):
        return jnp.einsum('gnh,gk->ghkn', m, eye).reshape(GROUP, S5_GROUPS * S5_N)
    def bd_out(m):
        return jnp.einsum('ghn,gk->gnkh', m, eye).reshape(S5_GROUPS * S5_N, GROUP)
    bbar = jnp.concatenate([bd_in(bbr), bd_in(bbi)], axis=1).astype(BF16)
    cmat = jnp.concatenate([bd_out(c_re.astype(F32)), -bd_out(c_im.astype(F32))],
                           axis=0).astype(BF16)
    return bbar, ar.reshape(1, -1), ai.reshape(1, -1), cmat


def _outproj_kernel(yf_ref, yh_ref, yd_ref, ys_ref, wo_ref, x_ref, pg_ref, g1_ref, g2_ref,
                    sc_ref, sh_ref, rw_ref, rb_ref,
                    x1_ref, h2_ref, idx_ref, wt_ref, cnt_ref, carry_ref):
    tm = x_ref.shape[0]

    @pl.when((pl.program_id(0) == 0) & (pl.program_id(1) == 0))
    def _():
        carry_ref[...] = jnp.zeros_like(carry_ref)

    y = (_dot(yf_ref[...], wo_ref[0:GROUP, :])
         + _dot(yh_ref[...], wo_ref[GROUP:2 * GROUP, :])
         + _dot(yd_ref[...], wo_ref[2 * GROUP:3 * GROUP, :])
         + _dot(ys_ref[...], wo_ref[3 * GROUP:4 * GROUP, :]))
    ms = jnp.mean(y * y, axis=-1, keepdims=True)
    x1 = x_ref[...] + g1_ref[0] * (y * lax.rsqrt(ms + EPS) * pg_ref[...])
    x1_ref[...] = x1
    ms2 = jnp.mean(x1 * x1, axis=-1, keepdims=True)
    h2 = (x1 * lax.rsqrt(ms2 + EPS) * g2_ref[...]) * (1.0 + sc_ref[0]) + sh_ref[0]
    for c in range(ROW_SPLIT):
        h2_ref[pl.ds(c, tm, stride=ROW_SPLIT), :] = h2[:, c * LANES:(c + 1) * LANES]
    h_hi = h2.astype(BF16)
    h_lo = (h2 - h_hi.astype(F32)).astype(BF16)
    rw = rw_ref[...]
    w_hi = rw.astype(BF16)
    w_lo = (rw - w_hi.astype(F32)).astype(BF16)
    logits = _dot(h_hi, w_hi) + (_dot(h_lo, w_hi) + _dot(h_hi, w_lo)) + rb_ref[...]
    lane = lax.broadcasted_iota(jnp.int32, (tm, LANES), 1)
    cur = logits
    vals, idxs = [], []
    for _ in range(TOP_K):
        m = jnp.max(cur, axis=-1, keepdims=True)
        sel = jnp.min(jnp.where(cur == m, lane, LANES), axis=-1, keepdims=True)
        vals.append(m)
        idxs.append(sel)
        cur = jnp.where(lane == sel, -jnp.inf, cur)
    es = [jnp.exp(v - vals[0]) for v in vals]
    tot = es[0] + es[1] + es[2] + es[3]
    hits = (lane == idxs[0]) | (lane == idxs[1]) | (lane == idxs[2]) | (lane == idxs[3])
    cnt = jnp.where(hits, 1.0, 0.0)
    r = lax.broadcasted_iota(jnp.int32, (tm, tm), 0)
    cidx = lax.broadcasted_iota(jnp.int32, (tm, tm), 1)
    below = jnp.where(cidx < r, 1.0, 0.0).astype(BF16)
    before = _dot(below, cnt.astype(BF16)) + carry_ref[...]
    total = carry_ref[...] + jnp.sum(cnt, axis=0, keepdims=True)
    carry_ref[...] = total
    cnt_ref[...] = total
    idx_out = jnp.zeros((tm, LANES), jnp.int32)
    wt_out = jnp.zeros((tm, LANES), F32)
    for k in range(TOP_K):
        rank = jnp.sum(jnp.where(lane == idxs[k], before, 0.0), axis=-1, keepdims=True)
        idx_out = jnp.where(lane == k, idxs[k], idx_out)
        idx_out = jnp.where(lane == TOP_K + k, rank.astype(jnp.int32), idx_out)
        wt_out = jnp.where(lane == k, es[k] / tot, wt_out)
    idx_ref[...] = idx_out
    wt_ref[...] = wt_out


def _outproj_call(yf, yh, yd, ys, w_out, x2, post_g, g1, pre_g2, sc2, sh2, rw, rb, batch, seq):
    t = x2.shape[0]
    tm = min(ROW_TILE, seq)
    nt = seq // tm
    row = lambda b, i: (b * nt + i, 0)
    const2 = lambda b, i: (0, 0)
    per_b = lambda b, i: (b, 0, 0)
    grp = pl.BlockSpec((tm, GROUP), row)
    vec = pl.BlockSpec((1, D_MODEL), const2)
    mod = pl.BlockSpec((1, 1, D_MODEL), per_b)
    return pl.pallas_call(
        _outproj_kernel,
        grid=(batch, nt),
        in_specs=[grp, grp, grp, grp,
                  pl.BlockSpec((D_MODEL, D_MODEL), const2),
                  pl.BlockSpec((tm, D_MODEL), row),
                  vec, mod, vec, mod, mod,
                  pl.BlockSpec((D_MODEL, LANES), const2),
                  pl.BlockSpec((1, LANES), const2)],
        out_specs=[pl.BlockSpec((tm, D_MODEL), row),
                   pl.BlockSpec((tm * ROW_SPLIT, LANES), row),
                   pl.BlockSpec((tm, LANES), row), pl.BlockSpec((tm, LANES), row),
                   pl.BlockSpec((1, LANES), const2)],
        out_shape=[jax.ShapeDtypeStruct((t, D_MODEL), F32),
                   jax.ShapeDtypeStruct((t * ROW_SPLIT, LANES), F32),
                   jax.ShapeDtypeStruct((t, LANES), jnp.int32),
                   jax.ShapeDtypeStruct((t, LANES), F32),
                   jax.ShapeDtypeStruct((1, LANES), F32)],
        scratch_shapes=[pltpu.VMEM((1, LANES), F32)],
        compiler_params=_cparams(("arbitrary", "arbitrary")),
    )(yf, yh, yd, ys, w_out, x2, post_g, g1, pre_g2, sc2, sh2, rw, rb)


def _row_copy(src_hbm, dst_vmem, src_row, dst_row, sem):
    s0 = pl.multiple_of(src_row * ROW_SPLIT, ROW_SPLIT)
    return pltpu.make_async_copy(src_hbm.at[pl.ds(s0, ROW_SPLIT), :],
                                 dst_vmem.at[pl.ds(dst_row * ROW_SPLIT, ROW_SPLIT), :], sem)


def _start_rows(idx_ref, n, src_hbm, dst_vmem, sem, idx_off=0):
    for j in range(n):
        _row_copy(src_hbm, dst_vmem, idx_ref[0, 0, idx_off + j], j, sem).start()


def _wait_rows(n, src_hbm, dst_vmem, sem):
    for j in range(n):
        _row_copy(src_hbm, dst_vmem, 0, j, sem).wait()


def _tile_rows(buf, n):
    return jnp.concatenate([buf[pl.ds(c, n, stride=ROW_SPLIT), :] for c in range(ROW_SPLIT)],
                           axis=1)


def _expert_kernel(te_ref, src_ref, nxt_ref, h_hbm, w1_ref, p_ref, b1_ref, w2_ref, b2_ref,
                   o_ref, xa, xb, w1_s, w2_s, sem):
    i = pl.program_id(0)
    last = pl.num_programs(0) - 1
    tm = EXP_TILE

    @pl.when(i == 0)
    def _():
        _start_rows(src_ref, tm, h_hbm, xa, sem.at[0])

    @pl.when((i == 0) | (te_ref[i] != te_ref[jnp.maximum(i - 1, 0)]))
    def _():
        chunk = 2 * LANES
        for c in range(2 * D_FF // chunk):
            r = _dot(w1_ref[0, 0, :, c * chunk:(c + 1) * chunk].astype(BF16), p_ref[...])
            w1_s[:, c * LANES:(c + 1) * LANES] = r[:, 0:LANES].astype(BF16)
            w1_s[:, D_FF + c * LANES:D_FF + (c + 1) * LANES] = r[:, LANES:chunk].astype(BF16)
        w2_s[...] = w2_ref[0, 0].astype(BF16)

    def step(cur, cur_sem, nxt, nxt_sem):
        _wait_rows(tm, h_hbm, cur, cur_sem)
        _start_rows(nxt_ref, tm, h_hbm, nxt, nxt_sem)
        hh = _dot(_tile_rows(cur, tm).astype(BF16), w1_s[...]) + b1_ref[0]
        glu = jnp.minimum(hh[:, 0:D_FF], SWIGLU_LIMIT)
        lin = jnp.clip(hh[:, D_FF:2 * D_FF], -SWIGLU_LIMIT, SWIGLU_LIMIT)
        act = glu * _sigmoid(SWIGLU_ALPHA * glu) * (lin + 1.0)
        y = _dot(act.astype(BF16), w2_s[...]) + b2_ref[0]
        for c in range(ROW_SPLIT):
            o_ref[pl.ds(c, tm, stride=ROW_SPLIT), :] = y[:, c * LANES:(c + 1) * LANES]

        @pl.when(i == last)
        def _():
            _wait_rows(tm, h_hbm, nxt, nxt_sem)

    @pl.when(lax.rem(i, 2) == 0)
    def _():
        step(xa, sem.at[0], xb, sem.at[1])

    @pl.when(lax.rem(i, 2) == 1)
    def _():
        step(xb, sem.at[1], xa, sem.at[0])


def _expert_call(tile_expert, src_rows, h2, w1, b1, w2, b2, layer):
    ntiles = tile_expert.shape[0]
    tm = EXP_TILE
    src3 = src_rows.reshape(ntiles, 1, tm)
    perm = np.zeros((2 * LANES, 2 * LANES), np.float32)
    perm[2 * np.arange(LANES), np.arange(LANES)] = 1.0
    perm[2 * np.arange(LANES) + 1, LANES + np.arange(LANES)] = 1.0
    grid_spec = pltpu.PrefetchScalarGridSpec(
        num_scalar_prefetch=1,
        grid=(ntiles,),
        in_specs=[pl.BlockSpec((1, 1, tm), lambda i, te: (i, 0, 0), memory_space=pltpu.SMEM),
                  pl.BlockSpec((1, 1, tm), lambda i, te: (jnp.minimum(i + 1, ntiles - 1), 0, 0),
                               memory_space=pltpu.SMEM),
                  pl.BlockSpec(memory_space=pl.ANY),
                  pl.BlockSpec((1, 1, D_MODEL, 2 * D_FF), lambda i, te: (layer, te[i], 0, 0)),
                  pl.BlockSpec((2 * LANES, 2 * LANES), lambda i, te: (0, 0)),
                  pl.BlockSpec((1, 1, 2 * D_FF), lambda i, te: (te[i], 0, 0)),
                  pl.BlockSpec((1, 1, D_FF, D_MODEL), lambda i, te: (layer, te[i], 0, 0)),
                  pl.BlockSpec((1, 1, D_MODEL), lambda i, te: (te[i], 0, 0))],
        out_specs=pl.BlockSpec((tm * ROW_SPLIT, LANES), lambda i, te: (i, 0)),
        scratch_shapes=[pltpu.VMEM((tm * ROW_SPLIT, LANES), F32),
                        pltpu.VMEM((tm * ROW_SPLIT, LANES), F32),
                        pltpu.VMEM((D_MODEL, 2 * D_FF), BF16),
                        pltpu.VMEM((D_FF, D_MODEL), BF16),
                        pltpu.SemaphoreType.DMA((2,))],
    )
    return pl.pallas_call(
        _expert_kernel,
        grid_spec=grid_spec,
        out_shape=jax.ShapeDtypeStruct((ntiles * tm * ROW_SPLIT, LANES), F32),
        compiler_params=_cparams(("arbitrary",)),
    )(tile_expert, src3, src3, h2, w1, jnp.asarray(perm, BF16), b1, w2, b2)


def _combine_kernel(pos_ref, nxt_ref, y_hbm, wt_ref, x_ref, pg_ref, g_ref, o_ref, ya, yb, sem):
    i = pl.program_id(0) * pl.num_programs(1) + pl.program_id(1)
    last = pl.num_programs(0) * pl.num_programs(1) - 1
    tm = x_ref.shape[0]

    def start(idx_ref, buf, s):
        for k in range(TOP_K):
            _start_rows(idx_ref, tm, y_hbm, buf.at[k], s, idx_off=k * tm)

    def wait(buf, s):
        for k in range(TOP_K):
            _wait_rows(tm, y_hbm, buf.at[k], s)

    @pl.when(i == 0)
    def _():
        start(pos_ref, ya, sem.at[0])

    def step(cur, cur_sem, nxt, nxt_sem):
        wait(cur, cur_sem)
        start(nxt_ref, nxt, nxt_sem)
        wt = wt_ref[...]
        y = wt[:, 0:1] * _tile_rows(cur.at[0], tm)
        for k in range(1, TOP_K):
            y = y + wt[:, k:k + 1] * _tile_rows(cur.at[k], tm)
        ms = jnp.mean(y * y, axis=-1, keepdims=True)
        o_ref[...] = x_ref[...] + g_ref[0] * (y * lax.rsqrt(ms + EPS) * pg_ref[...])

        @pl.when(i == last)
        def _():
            wait(nxt, nxt_sem)

    @pl.when(lax.rem(i, 2) == 0)
    def _():
        step(ya, sem.at[0], yb, sem.at[1])

    @pl.when(lax.rem(i, 2) == 1)
    def _():
        step(yb, sem.at[1], ya, sem.at[0])


def _combine_call(dest, y_sorted, wts, x1, post_g, g2, batch, seq):
    t = x1.shape[0]
    tm = min(CMB_TILE, seq)
    nt = seq // tm
    ntiles = t // tm
    dest_t = dest.reshape(ntiles, tm, TOP_K).transpose(0, 2, 1).reshape(ntiles, 1, TOP_K * tm)
    row = lambda b, i: (b * nt + i, 0)
    return pl.pallas_call(
        _combine_kernel,
        grid=(batch, nt),
        in_specs=[pl.BlockSpec((1, 1, TOP_K * tm), lambda b, i: (b * nt + i, 0, 0),
                               memory_space=pltpu.SMEM),
                  pl.BlockSpec((1, 1, TOP_K * tm),
                               lambda b, i: (jnp.minimum(b * nt + i + 1, ntiles - 1), 0, 0),
                               memory_space=pltpu.SMEM),
                  pl.BlockSpec(memory_space=pl.ANY),
                  pl.BlockSpec((tm, LANES), row),
                  pl.BlockSpec((tm, D_MODEL), row),
                  pl.BlockSpec((1, D_MODEL), lambda b, i: (0, 0)),
                  pl.BlockSpec((1, 1, D_MODEL), lambda b, i: (b, 0, 0))],
        out_specs=pl.BlockSpec((tm, D_MODEL), row),
        out_shape=jax.ShapeDtypeStruct((t, D_MODEL), F32),
        scratch_shapes=[pltpu.VMEM((TOP_K, tm * ROW_SPLIT, LANES), F32),
                        pltpu.VMEM((TOP_K, tm * ROW_SPLIT, LANES), F32),
                        pltpu.SemaphoreType.DMA((2,))],
        compiler_params=_cparams(("arbitrary", "arbitrary")),
    )(dest_t, dest_t, y_sorted, wts, x1, post_g, g2)


def _dispatch_plan(idx, rank, counts, t):
    tm = EXP_TILE
    npairs = t * TOP_K
    ntiles = npairs // tm + N_EXPERTS
    e_flat = idx.reshape(-1)
    pair = jnp.arange(npairs, dtype=jnp.int32)
    _, order = lax.sort((e_flat, pair), num_keys=1, is_stable=True)
    cnt_excl = jnp.cumsum(counts) - counts
    tiles_per = (counts + tm - 1) // tm
    tile_end = jnp.cumsum(tiles_per)
    tile_start = tile_end - tiles_per
    tile_ids = jnp.arange(ntiles, dtype=jnp.int32)
    tile_expert = jnp.minimum(
        jnp.sum((tile_ids[:, None] >= tile_end[None, :]).astype(jnp.int32), axis=1),
        N_EXPERTS - 1).astype(jnp.int32)
    rows = jnp.arange(ntiles * tm, dtype=jnp.int32)
    row_e = jnp.repeat(tile_expert, tm)
    off = rows - tile_start[row_e] * tm
    valid = (off < counts[row_e]) & (rows < tile_end[-1] * tm)
    sidx = jnp.clip(cnt_excl[row_e] + off, 0, npairs - 1)
    src_rows = jnp.where(valid, order[sidx] // TOP_K, 0).astype(jnp.int32)
    dest = (tile_start[idx] * tm + rank).astype(jnp.int32)
    return tile_expert, src_rows, dest


def _rope_tables():
    lane = np.arange(GROUP)
    d = lane % DIFF_D
    rd = DIFF_D // 4
    half = rd // 2
    inv = np.where(d < rd, ROPE_THETA ** (-(d % half).astype(np.float32) / half), 0.0)
    ma = np.where(d < half, -1.0, 0.0)
    mb = np.where((d >= half) & (d < rd), 1.0, 0.0)
    f = lambda a: jnp.asarray(a.reshape(1, GROUP), F32)
    return f(inv), f(ma), f(mb)


def kernel(x, c, positions, ada_w, ada_b, pre_norm_g, post_norm_g, w_in, w_out, fox_fb, hg_lower, hg_norm_g, diff_lam_q1, diff_lam_k1, diff_lam_q2, diff_lam_k2, diff_subln_g, s5_a_re, s5_a_im, s5_log_step, s5_b_re, s5_b_im, s5_c_re, s5_c_im, s5_d, s5_glu_w, router_w, router_b, exp_w1, exp_b1, exp_w2, exp_b2):
    batch, seq, d = x.shape
    depth = ada_w.shape[0]
    t = batch * seq
    assert d == D_MODEL and seq % ROW_TILE == 0

    lb_all = jnp.cumsum(jax.nn.softmax(hg_lower.astype(F32), axis=0), axis=0)
    lb_all = lb_all - lb_all[0:1]
    mod = _ada_call(c, ada_w, ada_b)
    inv_lane, mask_a, mask_b = _rope_tables()
    pos = positions.reshape(t, 1).astype(jnp.int32)
    xf = x.reshape(t, d)

    w_main, w_f = _win_prep_call(w_in)
    for l in range(depth):
        m6 = mod[l].reshape(batch, 6, 1, d)
        sh1, sc1, g1, sh2, sc2, g2 = [m6[:, j] for j in range(6)]
        fb = jnp.pad(fox_fb[l], (0, LANES - FOX_HEADS)).reshape(1, LANES)
        fox, frow, hg, diff, s5u = _inproj_call(
            xf, pre_norm_g[l, 0].reshape(1, d), sc1, sh1, w_main, w_f, fb, pos,
            inv_lane, mask_a, mask_b, batch, seq, l)

        y_fox = _fox_call(fox, frow, batch, seq)

        lam_init = 0.8 - 0.6 * math.exp(-0.3 * l)
        lam = (jnp.exp(jnp.sum(diff_lam_q1[l].astype(F32) * diff_lam_k1[l].astype(F32)))
               - jnp.exp(jnp.sum(diff_lam_q2[l].astype(F32) * diff_lam_k2[l].astype(F32))) + lam_init)
        y_diff = _diff_call(diff, lam.reshape(1), diff_subln_g[l].reshape(1, DIFF_DV),
                            1.0 - lam_init, batch, seq)

        y_hg = _hgrn2_call(hg, lb_all[l].reshape(1, -1), hg_norm_g[l].reshape(1, HG_DV), batch, seq)

        bbar, ar, ai, cmat = _s5_params(s5_a_re[l], s5_a_im[l], s5_log_step[l], s5_b_re[l],
                                        s5_b_im[l], s5_c_re[l], s5_c_im[l])
        u_sb = s5u.reshape(batch, seq, GROUP).transpose(1, 0, 2).reshape(t, GROUP)
        y_s5_sb = _s5_call(u_sb, bbar, ar, ai, cmat, s5_d[l].reshape(1, GROUP),
                           s5_glu_w[l].astype(BF16), batch, seq)
        y_s5 = y_s5_sb.reshape(seq, batch, GROUP).transpose(1, 0, 2).reshape(t, GROUP)

        rw = jnp.pad(router_w[l], ((0, 0), (0, LANES - N_EXPERTS)))
        rb = jnp.pad(router_b[l], (0, LANES - N_EXPERTS), constant_values=NEG_BIG).reshape(1, LANES)
        x1, h2, idx, wts, cnt = _outproj_call(
            y_fox, y_hg, y_diff, y_s5, w_out[l].astype(BF16), xf,
            post_norm_g[l, 0].reshape(1, d), g1, pre_norm_g[l, 1].reshape(1, d), sc2, sh2,
            rw, rb, batch, seq)

        tile_expert, src_rows, dest = _dispatch_plan(
            idx[:, 0:TOP_K], idx[:, TOP_K:2 * TOP_K], cnt[0, 0:N_EXPERTS].astype(jnp.int32), t)
        b1 = exp_b1[l]
        b1p = jnp.concatenate([b1[:, 0::2], b1[:, 1::2]], axis=-1).reshape(N_EXPERTS, 1, 2 * D_FF)
        y_sorted = _expert_call(tile_expert, src_rows, h2, exp_w1, b1p, exp_w2,
                                exp_b2[l].reshape(N_EXPERTS, 1, D_MODEL), l)
        xf = _combine_call(dest, y_sorted, wts, x1, post_norm_g[l, 1].reshape(1, d), g2, batch, seq)

    return xf.reshape(batch, seq, d)
```

```python
import functools
import math

import numpy as np
import jax
import jax.numpy as jnp
from jax import lax
from jax.experimental import pallas as pl
from jax.experimental.pallas import tpu as pltpu

F32 = jnp.float32
BF16 = jnp.bfloat16

D_MODEL = 1024
GROUP = 256
FOX_HEADS = 4
FOX_HD = 64
HG_HEADS = 4
HG_DK = 128
HG_DV = 64
DIFF_HEADS = 4
DIFF_DV = 64
DIFF_D = 32
S5_CH = 16
S5_GROUPS = 16
S5_N = 64
ROPE_THETA = 500000.0
N_EXPERTS = 32
TOP_K = 4
D_FF = 1024
SWIGLU_LIMIT = 7.0
SWIGLU_ALPHA = 1.702
EPS = 1e-6
NEG_BIG = -1e30

LANES = 128
VMEM_LIMIT = 56 * 1024 * 1024

LOG2E = 1.4426950408889634

ROW_TILE = 512
ATT_T = ROW_TILE
HG_CHUNK = 64
S5_STEPS = 128
EXP_TILE = 256
CMB_TILE = 256

ROW_SPLIT = D_MODEL // LANES

HG_COLS = 2 * HG_HEADS * HG_DK + 2 * GROUP
MAIN_COLS = 3 * GROUP + HG_COLS + 3 * GROUP + GROUP


def _cparams(sem):
    return pltpu.CompilerParams(dimension_semantics=sem, vmem_limit_bytes=VMEM_LIMIT)


def _sigmoid(x):
    return 1.0 / (1.0 + jnp.exp(-x))


def _shr(x, pow2):
    return lax.shift_right_logical(x, int(math.log2(pow2)))


def _dot(a, b):
    return jnp.dot(a, b, preferred_element_type=F32)


def _dot_nt(a, b):
    return lax.dot_general(a, b, (((1,), (1,)), ((), ())), preferred_element_type=F32)


def _dot_tn(a, b):
    return lax.dot_general(a, b, (((0,), (0,)), ((), ())), preferred_element_type=F32)


def _split_dot(m_bf16, x, terms=3):
    out = None
    for _ in range(terms):
        piece = x.astype(BF16)
        x = x - piece.astype(F32)
        d = _dot(m_bf16, piece)
        out = d if out is None else out + d
    return out


def _ada_kernel(c_ref, w_ref, b_ref, o_ref):
    c = c_ref[...]
    cs = c * _sigmoid(c)
    o_ref[0] = jnp.dot(cs, w_ref[0], preferred_element_type=F32,
                       precision=lax.Precision.HIGHEST) + b_ref[0]


def _ada_call(c, ada_w, ada_b):
    depth, d, n = ada_w.shape
    b = c.shape[0]
    tn = 1536
    return pl.pallas_call(
        _ada_kernel,
        grid=(depth, n // tn),
        in_specs=[pl.BlockSpec((b, d), lambda l, j: (0, 0)),
                  pl.BlockSpec((1, d, tn), lambda l, j: (l, 0, j)),
                  pl.BlockSpec((1, 1, tn), lambda l, j: (l, 0, j))],
        out_specs=pl.BlockSpec((1, b, tn), lambda l, j: (l, 0, j)),
        out_shape=jax.ShapeDtypeStruct((depth, b, n), F32),
        compiler_params=_cparams(("arbitrary", "arbitrary")),
    )(c, ada_w, ada_b.reshape(depth, 1, n))


_FOX_COLS = 3 * GROUP


def _win_prep_kernel(w_ref, main_ref, f_ref):
    rows = w_ref.shape[1]
    lane = lax.broadcasted_iota(jnp.int32, (rows, LANES), 1)
    main_ref[0, :, 0:_FOX_COLS] = w_ref[0, :, 0:_FOX_COLS].astype(BF16)
    first = w_ref[0, :, _FOX_COLS:_FOX_COLS + LANES]
    f_ref[0] = jnp.where(lane < FOX_HEADS, first, 0.0).astype(BF16)
    cur = pltpu.roll(first, LANES - FOX_HEADS, axis=1)
    for m in range((MAIN_COLS - _FOX_COLS) // LANES):
        c0 = _FOX_COLS + m * LANES
        nxt = pltpu.roll(w_ref[0, :, c0 + LANES:c0 + 2 * LANES], LANES - FOX_HEADS, axis=1)
        main_ref[0, :, c0:c0 + LANES] = jnp.where(lane < LANES - FOX_HEADS, cur, nxt).astype(BF16)
        cur = nxt


def _win_prep_call(w_in):
    depth, d, cols = w_in.shape
    padded = MAIN_COLS + LANES
    w_pad = jnp.pad(w_in, ((0, 0), (0, 0), (0, padded - cols)))
    rows = 256
    return pl.pallas_call(
        _win_prep_kernel,
        grid=(depth, d // rows),
        in_specs=[pl.BlockSpec((1, rows, padded), lambda l, r: (l, r, 0))],
        out_specs=[pl.BlockSpec((1, rows, MAIN_COLS), lambda l, r: (l, r, 0)),
                   pl.BlockSpec((1, rows, LANES), lambda l, r: (l, r, 0))],
        out_shape=[jax.ShapeDtypeStruct((depth, d, MAIN_COLS), BF16),
                   jax.ShapeDtypeStruct((depth, d, LANES), BF16)],
        compiler_params=_cparams(("arbitrary", "arbitrary")),
    )(w_pad)


def _inproj_kernel(x_ref, g_ref, sc_ref, sh_ref, w_ref, wf_ref, fb_ref, pos_ref, inv_ref,
                   ma_ref, mb_ref,
                   fox_ref, frow_ref, hg_ref, diff_ref, s5_ref, carry_ref):
    i = pl.program_id(1)
    tm = x_ref.shape[0]

    @pl.when(i == 0)
    def _():
        carry_ref[...] = jnp.zeros_like(carry_ref)

    x = x_ref[...]
    ms = jnp.mean(x * x, axis=-1, keepdims=True)
    h = (x * lax.rsqrt(ms + EPS) * g_ref[...]) * (1.0 + sc_ref[0]) + sh_ref[0]
    hb = h.astype(BF16)

    o = 0
    pf = _dot(hb, w_ref[0, :, o:o + 3 * GROUP])
    fox_ref[:, 0:GROUP] = (pf[:, 0:GROUP] * (FOX_HD ** -0.5 * LOG2E)).astype(BF16)
    fox_ref[:, GROUP:3 * GROUP] = pf[:, GROUP:3 * GROUP].astype(BF16)
    o += 3 * GROUP

    ff = _dot(hb, wf_ref[0]) + fb_ref[...]
    logf = jnp.minimum(ff, 0.0) - jnp.log(1.0 + jnp.exp(-jnp.abs(ff)))
    r = lax.broadcasted_iota(jnp.int32, (tm, tm), 0)
    cidx = lax.broadcasted_iota(jnp.int32, (tm, tm), 1)
    tri = jnp.where(cidx <= r, 1.0, 0.0).astype(BF16)
    cum = _split_dot(tri, logf) + carry_ref[...]
    carry_ref[...] = cum[tm - 1:tm, :]
    cum_t = jnp.transpose(cum * LOG2E)
    frow_ref[0, 0] = cum_t[0:8, :]

    hg_ref[...] = _dot(hb, w_ref[0, :, o:o + HG_COLS]).astype(BF16)
    o += HG_COLS

    pd = _dot(hb, w_ref[0, :, o:o + 3 * GROUP])
    o += 3 * GROUP
    pos = pos_ref[...].astype(F32)
    ang = pos * inv_ref[...]
    cos = jnp.cos(ang)
    sin = jnp.sin(ang)
    ma = ma_ref[...]
    mb = mb_ref[...]

    def rope(xx):
        outs = []
        for j in range(GROUP // LANES):
            sl = slice(j * LANES, (j + 1) * LANES)
            xs = xx[:, sl]
            partner = (pltpu.roll(xs, LANES - 4, axis=1) * ma[:, sl]
                       + pltpu.roll(xs, 4, axis=1) * mb[:, sl])
            outs.append(xs * cos[:, sl] + partner * sin[:, sl])
        return jnp.concatenate(outs, axis=1)

    diff_ref[:, 0:GROUP] = (rope(pd[:, 0:GROUP]) * (DIFF_D ** -0.5 * LOG2E)).astype(BF16)
    diff_ref[:, GROUP:2 * GROUP] = rope(pd[:, GROUP:2 * GROUP]).astype(BF16)
    diff_ref[:, 2 * GROUP:3 * GROUP] = pd[:, 2 * GROUP:3 * GROUP].astype(BF16)

    s5_ref[...] = _dot(hb, w_ref[0, :, o:o + GROUP]).astype(BF16)


def _inproj_call(x2, g_pre, sc, sh, w_main, w_f, fb, pos, inv_lane, mask_a, mask_b, batch, seq,
                 layer):
    t = x2.shape[0]
    tm = ROW_TILE
    nt = seq // tm
    row = lambda b, i: (b * nt + i, 0)
    const2 = lambda b, i: (0, 0)
    per_b = lambda b, i: (b, 0, 0)
    outs = pl.pallas_call(
        _inproj_kernel,
        grid=(batch, nt),
        in_specs=[pl.BlockSpec((tm, D_MODEL), row),
                  pl.BlockSpec((1, D_MODEL), const2),
                  pl.BlockSpec((1, 1, D_MODEL), per_b),
                  pl.BlockSpec((1, 1, D_MODEL), per_b),
                  pl.BlockSpec((1, D_MODEL, MAIN_COLS), lambda b, i: (layer, 0, 0)),
                  pl.BlockSpec((1, D_MODEL, LANES), lambda b, i: (layer, 0, 0)),
                  pl.BlockSpec((1, LANES), const2),
                  pl.BlockSpec((tm, 1), row),
                  pl.BlockSpec((1, GROUP), const2),
                  pl.BlockSpec((1, GROUP), const2),
                  pl.BlockSpec((1, GROUP), const2)],
        out_specs=[pl.BlockSpec((tm, 3 * GROUP), row),
                   pl.BlockSpec((1, 1, 8, tm), lambda b, i: (b, i, 0, 0)),
                   pl.BlockSpec((tm, HG_COLS), row),
                   pl.BlockSpec((tm, 3 * GROUP), row),
                   pl.BlockSpec((tm, GROUP), row)],
        out_shape=[jax.ShapeDtypeStruct((t, 3 * GROUP), BF16),
                   jax.ShapeDtypeStruct((batch, nt, 8, tm), F32),
                   jax.ShapeDtypeStruct((t, HG_COLS), BF16),
                   jax.ShapeDtypeStruct((t, 3 * GROUP), BF16),
                   jax.ShapeDtypeStruct((t, GROUP), BF16)],
        scratch_shapes=[pltpu.VMEM((1, LANES), F32)],
        compiler_params=_cparams(("arbitrary", "arbitrary")),
    )(x2, g_pre, sc, sh, w_main, w_f, fb, pos, inv_lane, mask_a, mask_b)
    return outs


_HEAD_W = 64
_HEADS = 4


def _slot(x64):
    return jnp.concatenate([x64, jnp.zeros_like(x64)], axis=1)


def _pad_kv(k_ref, v_ref, k_s, v_s):
    seq = k_ref.shape[0]
    rows = ATT_T
    lane = lax.broadcasted_iota(jnp.int32, (rows, LANES), 1)

    def body(c, _):
        r0 = pl.multiple_of(c * rows, rows)
        for h in range(_HEADS):
            k_s[pl.ds(r0, rows), h * LANES:(h + 1) * LANES] = _slot(
                k_ref[pl.ds(r0, rows), h * _HEAD_W:(h + 1) * _HEAD_W])
            v = _slot(v_ref[pl.ds(r0, rows), h * _HEAD_W:(h + 1) * _HEAD_W])
            v_s[pl.ds(r0, rows), h * LANES:(h + 1) * LANES] = jnp.where(
                lane == _HEAD_W, jnp.ones_like(v), v)
        return 0

    lax.fori_loop(0, seq // rows, body, 0)


def _attend(qs, slots, scratch, qi, key_bias):
    k_s, v_s = scratch
    n = len(qs)
    tq = qs[0].shape[0]
    tk = ATT_T

    def block(kb, carry, keep):
        ms, accs = carry
        start = pl.multiple_of(kb * tk, tk)
        new_m, new_acc = [], []
        for j in range(n):
            c0 = slots[j] * LANES
            s = _dot_nt(qs[j], k_s[pl.ds(start, tk), c0:c0 + LANES])
            kbias = key_bias(j, kb)
            if kbias is not None:
                s = s - kbias
            if keep is not None:
                s = jnp.where(keep, s, NEG_BIG)
            m_new = jnp.maximum(ms[j], jnp.max(s, axis=-1, keepdims=True))
            alpha = jnp.exp2(ms[j] - m_new)
            p = jnp.exp2(s - m_new).astype(BF16)
            new_acc.append(alpha * accs[j] + _dot(p, v_s[pl.ds(start, tk), c0:c0 + LANES]))
            new_m.append(m_new)
        return tuple(new_m), tuple(new_acc)

    init = (tuple(jnp.full((tq, 1), NEG_BIG, F32) for _ in range(n)),
            tuple(jnp.zeros((tq, LANES), F32) for _ in range(n)))
    carry = lax.fori_loop(0, qi, lambda kb, c: block(kb, c, None), init)
    causal = (lax.broadcasted_iota(jnp.int32, (tq, tk), 1)
              <= lax.broadcasted_iota(jnp.int32, (tq, tk), 0))
    return block(qi, carry, causal)[1]


def _fox_kernel(q_ref, k_ref, v_ref, frow_ref, o_ref, k_s, v_s):
    qi = pl.program_id(1)

    @pl.when(qi == 0)
    def _():
        _pad_kv(k_ref, v_ref, k_s, v_s)

    qs = [_slot(q_ref[:, h * _HEAD_W:(h + 1) * _HEAD_W]) for h in range(_HEADS)]
    accs = _attend(qs, list(range(_HEADS)), (k_s, v_s), qi,
                   lambda j, kb: frow_ref[0, kb, j:j + 1, :])
    outs = [a[:, 0:_HEAD_W] / a[:, _HEAD_W:_HEAD_W + 1] for a in accs]
    o_ref[...] = jnp.concatenate(outs, axis=1).astype(BF16)


def _att_specs(seq):
    nq = seq // ATT_T
    return ([pl.BlockSpec((ATT_T, GROUP), lambda b, i: (b * nq + i, 0)),
             pl.BlockSpec((seq, GROUP), lambda b, i: (b, 1)),
             pl.BlockSpec((seq, GROUP), lambda b, i: (b, 2))],
            pl.BlockSpec((ATT_T, GROUP), lambda b, i: (b * nq + i, 0)),
            [pltpu.VMEM((seq, _HEADS * LANES), BF16), pltpu.VMEM((seq, _HEADS * LANES), BF16)])


def _fox_call(fox, frow, batch, seq):
    t = fox.shape[0]
    nq = seq // ATT_T
    qkv_specs, out_spec, scratch = _att_specs(seq)
    return pl.pallas_call(
        _fox_kernel,
        grid=(batch, nq),
        in_specs=qkv_specs + [pl.BlockSpec((1, nq, 8, ATT_T), lambda b, i: (b, 0, 0, 0))],
        out_specs=out_spec,
        out_shape=jax.ShapeDtypeStruct((t, GROUP), BF16),
        scratch_shapes=scratch,
        compiler_params=_cparams(("arbitrary", "arbitrary")),
    )(fox, fox, fox, frow)


def _diff_kernel(lam_ref, q_ref, k_ref, v_ref, g_ref, o_ref, k_s, v_s, *, out_scale):
    qi = pl.program_id(1)

    @pl.when(qi == 0)
    def _():
        _pad_kv(k_ref, v_ref, k_s, v_s)

    lam = lam_ref[0]
    lane = lax.broadcasted_iota(jnp.int32, (ATT_T, LANES), 1)
    outs = []
    for pair in range(_HEADS // 2):
        qs, slots = [], []
        for h in (2 * pair, 2 * pair + 1):
            q = _slot(q_ref[:, h * _HEAD_W:(h + 1) * _HEAD_W])
            qs += [jnp.where(lane < DIFF_D, q, jnp.zeros_like(q)),
                   jnp.where(lane >= DIFF_D, q, jnp.zeros_like(q))]
            slots += [h, h]
        accs = _attend(qs, slots, (k_s, v_s), qi, lambda j, kb: None)
        for j in (0, 2):
            y = (accs[j][:, 0:_HEAD_W] / accs[j][:, _HEAD_W:_HEAD_W + 1]
                 - lam * (accs[j + 1][:, 0:_HEAD_W] / accs[j + 1][:, _HEAD_W:_HEAD_W + 1]))
            ms = jnp.mean(y * y, axis=-1, keepdims=True)
            outs.append(y * lax.rsqrt(ms + EPS) * g_ref[...] * out_scale)
    o_ref[...] = jnp.concatenate(outs, axis=1).astype(BF16)


def _diff_call(diff, lam, subln_g, out_scale, batch, seq):
    t = diff.shape[0]
    nq = seq // ATT_T
    qkv_specs, out_spec, scratch = _att_specs(seq)
    return pl.pallas_call(
        functools.partial(_diff_kernel, out_scale=out_scale),
        grid=(batch, nq),
        in_specs=([pl.BlockSpec(memory_space=pltpu.SMEM)] + qkv_specs
                  + [pl.BlockSpec((1, DIFF_DV), lambda b, i: (0, 0))]),
        out_specs=out_spec,
        out_shape=jax.ShapeDtypeStruct((t, GROUP), BF16),
        scratch_shapes=scratch,
        compiler_params=_cparams(("arbitrary", "arbitrary")),
    )(lam, diff, diff, diff, subln_g)


_HG_LEVELS = (32, 16, 8)


def _hg_ref_rows(G, level):
    C = G.shape[0]
    rows = []
    for tile in range(C // 8):
        if level is None:
            r = tile * 8 + 3
        else:
            grp = (tile * 8) // (2 * level)
            r = grp * 2 * level + level - 1
        rows.append(jnp.broadcast_to(G[r:r + 1, :], (8, G.shape[1])))
    return jnp.concatenate(rows, axis=0)


def _hgrn2_kernel(p_ref, lb_ref, ng_ref, o_ref, state_ref):
    i = pl.program_id(1)
    tm = p_ref.shape[0]
    C = HG_CHUNK
    KW = HG_HEADS * HG_DK

    @pl.when(i == 0)
    def _():
        state_ref[...] = jnp.zeros_like(state_ref)

    lb = lb_ref[...]
    rr = lax.broadcasted_iota(jnp.int32, (C, C), 0)
    cc = lax.broadcasted_iota(jnp.int32, (C, C), 1)
    tri = jnp.where(cc <= rr, 1.0, 0.0).astype(BF16)
    level_masks = []
    for m in _HG_LEVELS:
        same_grp = _shr(rr, 2 * m) == _shr(cc, 2 * m)
        level_masks.append(same_grp & ((rr & (2 * m - 1)) >= m) & ((cc & (2 * m - 1)) < m))
    diag_mask = (_shr(rr, 8) == _shr(cc, 8)) & (cc <= rr)
    row_idx = lax.broadcasted_iota(jnp.int32, (C, HG_DK), 0)

    def chunk(base, states):
        hq = p_ref[pl.ds(base, C), 0:KW].astype(F32)
        q_all = hq * _sigmoid(hq)
        f = lb + (1.0 - lb) * _sigmoid(p_ref[pl.ds(base, C), KW:2 * KW].astype(F32))
        k_all = 1.0 - f
        g_all = _split_dot(tri, jnp.log(f), terms=2)
        new_states = []
        for h in range(HG_HEADS):
            ks = slice(h * HG_DK, (h + 1) * HG_DK)
            q = q_all[:, ks]
            k = k_all[:, ks]
            G = g_all[:, ks]
            v = p_ref[pl.ds(base, C), 2 * KW + h * HG_DV:2 * KW + (h + 1) * HG_DV]
            st = states[h]
            gl = G[C - 1:C, :]
            o = _dot_nt((q * jnp.exp(G)).astype(BF16), st.astype(BF16))
            a = jnp.zeros((C, C), F32)
            for m, mask in zip(_HG_LEVELS, level_masks):
                ref = _hg_ref_rows(G, m)
                upper = (row_idx & (2 * m - 1)) >= m
                e = jnp.exp(jnp.where(upper, G - ref, ref - G))
                am = _dot_nt((q * e).astype(BF16), (k * e).astype(BF16))
                a = a + jnp.where(mask, am, 0.0)
            ref = _hg_ref_rows(G, None)
            d = jnp.clip(G - ref, -80.0, 80.0)
            am = _dot_nt((q * jnp.exp(d)).astype(BF16), (k * jnp.exp(-d)).astype(BF16))
            a = a + jnp.where(diag_mask, am, 0.0)
            o = o + _dot(a.astype(BF16), v)
            kd = (k * jnp.exp(gl - G)).astype(BF16)
            new_states.append(jnp.exp(gl) * st + _dot_tn(v, kd))
            ms = jnp.mean(o * o, axis=-1, keepdims=True)
            on = o * lax.rsqrt(ms + EPS) * ng_ref[...]
            gate = p_ref[pl.ds(base, C), 2 * KW + GROUP + h * HG_DV:
                         2 * KW + GROUP + (h + 1) * HG_DV].astype(F32)
            o_ref[pl.ds(base, C), h * HG_DV:(h + 1) * HG_DV] = (
                on * (gate * _sigmoid(gate))).astype(BF16)
        return tuple(new_states)

    def pair(ci, states):
        first = pl.multiple_of(ci * 2 * C, 2 * C)
        second = pl.multiple_of(ci * 2 * C + C, C)
        return chunk(second, chunk(first, states))

    states = lax.fori_loop(0, tm // (2 * C), pair,
                           tuple(state_ref[h] for h in range(HG_HEADS)))
    for h in range(HG_HEADS):
        state_ref[h] = states[h]


def _hgrn2_call(hg, lb, norm_g, batch, seq):
    t = hg.shape[0]
    tm = min(ROW_TILE, seq)
    nt = seq // tm
    kw = HG_HEADS * HG_DK
    return pl.pallas_call(
        _hgrn2_kernel,
        grid=(batch, nt),
        in_specs=[pl.BlockSpec((tm, HG_COLS), lambda b, i: (b * nt + i, 0)),
                  pl.BlockSpec((1, kw), lambda b, i: (0, 0)),
                  pl.BlockSpec((1, HG_DV), lambda b, i: (0, 0))],
        out_specs=pl.BlockSpec((tm, GROUP), lambda b, i: (b * nt + i, 0)),
        out_shape=jax.ShapeDtypeStruct((t, GROUP), BF16),
        scratch_shapes=[pltpu.VMEM((HG_HEADS, HG_DV, HG_DK), F32)],
        compiler_params=_cparams(("arbitrary", "arbitrary")),
    )(hg, lb, norm_g)


def _s5_kernel(u_ref, bb_ref, ar_ref, ai_ref, cm_ref, d_ref, glu_ref, o_ref, bu_s, xr_s, xi_s,
               *, batch):
    i = pl.program_id(0)
    nst = S5_GROUPS * S5_N
    steps = u_ref.shape[0] // batch

    @pl.when(i == 0)
    def _():
        xr_s[...] = jnp.zeros_like(xr_s)
        xi_s[...] = jnp.zeros_like(xi_s)

    u = u_ref[...]
    bu_s[...] = _dot(u, bb_ref[...])
    ar = jnp.broadcast_to(ar_ref[...], (batch, nst))
    ai = jnp.broadcast_to(ai_ref[...], (batch, nst))

    def step(t, carry):
        xr, xi = carry
        r0 = pl.multiple_of(t * batch, batch)
        nr = ar * xr - ai * xi + bu_s[pl.ds(r0, batch), 0:nst]
        ni = ar * xi + ai * xr + bu_s[pl.ds(r0, batch), nst:2 * nst]
        bu_s[pl.ds(r0, batch), 0:nst] = nr
        bu_s[pl.ds(r0, batch), nst:2 * nst] = ni
        return nr, ni

    xr, xi = lax.fori_loop(0, steps, step, (xr_s[...], xi_s[...]))
    xr_s[...] = xr
    xi_s[...] = xi
    y = _dot(bu_s[...].astype(BF16), cm_ref[...]) + d_ref[...] * u.astype(F32)
    gel = 0.5 * y * (1.0 + jnp.tanh(0.7978845608028654 * (y + 0.044715 * (y * y * y))))
    z = _dot(gel.astype(BF16), glu_ref[...])
    o_ref[...] = (z[:, 0:GROUP] * _sigmoid(z[:, GROUP:2 * GROUP])).astype(BF16)


def _s5_call(u_sb, bbar, ar, ai, cmat, dskip, glu_w, batch, seq):
    rows = u_sb.shape[0]
    steps = min(S5_STEPS, seq)
    tr = steps * batch
    nst = S5_GROUPS * S5_N
    const = lambda i: (0, 0)
    return pl.pallas_call(
        functools.partial(_s5_kernel, batch=batch),
        grid=(seq // steps,),
        in_specs=[pl.BlockSpec((tr, GROUP), lambda i: (i, 0)),
                  pl.BlockSpec((GROUP, 2 * nst), const),
                  pl.BlockSpec((1, nst), const),
                  pl.BlockSpec((1, nst), const),
                  pl.BlockSpec((2 * nst, GROUP), const),
                  pl.BlockSpec((1, GROUP), const),
                  pl.BlockSpec((GROUP, 2 * GROUP), const)],
        out_specs=pl.BlockSpec((tr, GROUP), lambda i: (i, 0)),
        out_shape=jax.ShapeDtypeStruct((rows, GROUP), BF16),
        scratch_shapes=[pltpu.VMEM((tr, 2 * nst), F32),
                        pltpu.VMEM((batch, nst), F32), pltpu.VMEM((batch, nst), F32)],
        compiler_params=_cparams(("arbitrary",)),
    )(u_sb, bbar, ar, ai, cmat, dskip, glu_w)


def _s5_params(a_re, a_im, log_step, b_re, b_im, c_re, c_im):
    lr = jnp.minimum(a_re.astype(F32), -1e-4)
    li = a_im.astype(F32)
    dt = jnp.exp(log_step.astype(F32))[:, None]
    mag = jnp.exp(lr * dt)
    ar = mag * jnp.cos(li * dt)
    ai = mag * jnp.sin(li * dt)
    den = lr * lr + li * li
    zr = ((ar - 1.0) * lr + ai * li) / den
    zi = (ai * lr - (ar - 1.0) * li) / den
    br = b_re.astype(F32)
    bi = b_im.astype(F32)
    bbr = zr[..., None] * br - zi[..., None] * bi
    bbi = zr[..., None] * bi + zi[..., None] * br
    eye = jnp.eye(S5_GROUPS, dtype=F32)
    def bd_in(m):
        return jnp.einsum('gnh,gk->ghkn', m, eye).reshape(GROUP, S5_GROUPS * S5_N)
    def bd_out(m):
        return jnp.einsum('ghn,gk->gnkh', m, eye).reshape(S5_GROUPS * S5_N, GROUP)
    bbar = jnp.concatenate([bd_in(bbr), bd_in(bbi)], axis=1).astype(BF16)
    cmat = jnp.concatenate([bd_out(c_re.astype(F32)), -bd_out(c_im.astype(F32))],
                           axis=0).astype(BF16)
    return bbar, ar.reshape(1, -1), ai.reshape(1, -1), cmat


def _outproj_kernel(yf_ref, yh_ref, yd_ref, ys_ref, wo_ref, x_ref, pg_ref, g1_ref, g2_ref,
                    sc_ref, sh_ref, rw_ref, rb_ref,
                    x1_ref, h2_ref, idx_ref, wt_ref, cnt_ref, carry_ref):
    tm = x_ref.shape[0]

    @pl.when((pl.program_id(0) == 0) & (pl.program_id(1) == 0))
    def _():
        carry_ref[...] = jnp.zeros_like(carry_ref)

    y = (_dot(yf_ref[...], wo_ref[0:GROUP, :])
         + _dot(yh_ref[...], wo_ref[GROUP:2 * GROUP, :])
         + _dot(yd_ref[...], wo_ref[2 * GROUP:3 * GROUP, :])
         + _dot(ys_ref[...], wo_ref[3 * GROUP:4 * GROUP, :]))
    ms = jnp.mean(y * y, axis=-1, keepdims=True)
    x1 = x_ref[...] + g1_ref[0] * (y * lax.rsqrt(ms + EPS) * pg_ref[...])
    x1_ref[...] = x1
    ms2 = jnp.mean(x1 * x1, axis=-1, keepdims=True)
    h2 = (x1 * lax.rsqrt(ms2 + EPS) * g2_ref[...]) * (1.0 + sc_ref[0]) + sh_ref[0]
    for c in range(ROW_SPLIT):
        h2_ref[pl.ds(c, tm, stride=ROW_SPLIT), :] = h2[:, c * LANES:(c + 1) * LANES]
    h_hi = h2.astype(BF16)
    h_lo = (h2 - h_hi.astype(F32)).astype(BF16)
    rw = rw_ref[...]
    w_hi = rw.astype(BF16)
    w_lo = (rw - w_hi.astype(F32)).astype(BF16)
    logits = _dot(h_hi, w_hi) + (_dot(h_lo, w_hi) + _dot(h_hi, w_lo)) + rb_ref[...]
    lane = lax.broadcasted_iota(jnp.int32, (tm, LANES), 1)
    cur = logits
    vals, idxs = [], []
    for _ in range(TOP_K):
        m = jnp.max(cur, axis=-1, keepdims=True)
        sel = jnp.min(jnp.where(cur == m, lane, LANES), axis=-1, keepdims=True)
        vals.append(m)
        idxs.append(sel)
        cur = jnp.where(lane == sel, -jnp.inf, cur)
    es = [jnp.exp(v - vals[0]) for v in vals]
    tot = es[0] + es[1] + es[2] + es[3]
    hits = (lane == idxs[0]) | (lane == idxs[1]) | (lane == idxs[2]) | (lane == idxs[3])
    cnt = jnp.where(hits, 1.0, 0.0)
    r = lax.broadcasted_iota(jnp.int32, (tm, tm), 0)
    cidx = lax.broadcasted_iota(jnp.int32, (tm, tm), 1)
    below = jnp.where(cidx < r, 1.0, 0.0).astype(BF16)
    before = _dot(below, cnt.astype(BF16)) + carry_ref[...]
    total = carry_ref[...] + jnp.sum(cnt, axis=0, keepdims=True)
    carry_ref[...] = total
    cnt_ref[...] = total
    idx_out = jnp.zeros((tm, LANES), jnp.int32)
    wt_out = jnp.zeros((tm, LANES), F32)
    for k in range(TOP_K):
        rank = jnp.sum(jnp.where(lane == idxs[k], before, 0.0), axis=-1, keepdims=True)
        idx_out = jnp.where(lane == k, idxs[k], idx_out)
        idx_out = jnp.where(lane == TOP_K + k, rank.astype(jnp.int32), idx_out)
        wt_out = jnp.where(lane == k, es[k] / tot, wt_out)
    idx_ref[...] = idx_out
    wt_ref[...] = wt_out


def _outproj_call(yf, yh, yd, ys, w_out, x2, post_g, g1, pre_g2, sc2, sh2, rw, rb, batch, seq):
    t = x2.shape[0]
    tm = min(ROW_TILE, seq)
    nt = seq // tm
    row = lambda b, i: (b * nt + i, 0)
    const2 = lambda b, i: (0, 0)
    per_b = lambda b, i: (b, 0, 0)
    grp = pl.BlockSpec((tm, GROUP), row)
    vec = pl.BlockSpec((1, D_MODEL), const2)
    mod = pl.BlockSpec((1, 1, D_MODEL), per_b)
    return pl.pallas_call(
        _outproj_kernel,
        grid=(batch, nt),
        in_specs=[grp, grp, grp, grp,
                  pl.BlockSpec((D_MODEL, D_MODEL), const2),
                  pl.BlockSpec((tm, D_MODEL), row),
                  vec, mod, vec, mod, mod,
                  pl.BlockSpec((D_MODEL, LANES), const2),
                  pl.BlockSpec((1, LANES), const2)],
        out_specs=[pl.BlockSpec((tm, D_MODEL), row),
                   pl.BlockSpec((tm * ROW_SPLIT, LANES), row),
                   pl.BlockSpec((tm, LANES), row), pl.BlockSpec((tm, LANES), row),
                   pl.BlockSpec((1, LANES), const2)],
        out_shape=[jax.ShapeDtypeStruct((t, D_MODEL), F32),
                   jax.ShapeDtypeStruct((t * ROW_SPLIT, LANES), F32),
                   jax.ShapeDtypeStruct((t, LANES), jnp.int32),
                   jax.ShapeDtypeStruct((t, LANES), F32),
                   jax.ShapeDtypeStruct((1, LANES), F32)],
        scratch_shapes=[pltpu.VMEM((1, LANES), F32)],
        compiler_params=_cparams(("arbitrary", "arbitrary")),
    )(yf, yh, yd, ys, w_out, x2, post_g, g1, pre_g2, sc2, sh2, rw, rb)


def _row_copy(src_hbm, dst_vmem, src_row, dst_row, sem):
    s0 = pl.multiple_of(src_row * ROW_SPLIT, ROW_SPLIT)
    return pltpu.make_async_copy(src_hbm.at[pl.ds(s0, ROW_SPLIT), :],
                                 dst_vmem.at[pl.ds(dst_row * ROW_SPLIT, ROW_SPLIT), :], sem)


def _start_rows(idx_ref, n, src_hbm, dst_vmem, sem, idx_off=0):
    for j in range(n):
        _row_copy(src_hbm, dst_vmem, idx_ref[0, 0, idx_off + j], j, sem).start()


def _wait_rows(n, src_hbm, dst_vmem, sem):
    pltpu.make_async_copy(src_hbm.at[pl.ds(0, n * ROW_SPLIT), :], dst_vmem, sem).wait()


def _tile_rows(buf, n):
    return jnp.concatenate([buf[pl.ds(c, n, stride=ROW_SPLIT), :] for c in range(ROW_SPLIT)],
                           axis=1)


def _expert_kernel(te_ref, src_ref, nxt_ref, h_hbm, w1_ref, p_ref, b1_ref, w2_ref, b2_ref,
                   o_ref, xa, xb, w1_s, w2_s, sem):
    i = pl.program_id(0)
    last = pl.num_programs(0) - 1
    tm = EXP_TILE

    @pl.when(i == 0)
    def _():
        _start_rows(src_ref, tm, h_hbm, xa, sem.at[0])

    @pl.when((i == 0) | (te_ref[i] != te_ref[jnp.maximum(i - 1, 0)]))
    def _():
        chunk = 2 * LANES
        for c in range(2 * D_FF // chunk):
            r = _dot(w1_ref[0, 0, :, c * chunk:(c + 1) * chunk].astype(BF16), p_ref[...])
            w1_s[:, c * LANES:(c + 1) * LANES] = r[:, 0:LANES].astype(BF16)
            w1_s[:, D_FF + c * LANES:D_FF + (c + 1) * LANES] = r[:, LANES:chunk].astype(BF16)
        w2_s[...] = w2_ref[0, 0].astype(BF16)

    def step(cur, cur_sem, nxt, nxt_sem):
        _wait_rows(tm, h_hbm, cur, cur_sem)
        _start_rows(nxt_ref, tm, h_hbm, nxt, nxt_sem)
        hh = _dot(_tile_rows(cur, tm).astype(BF16), w1_s[...]) + b1_ref[0]
        glu = jnp.minimum(hh[:, 0:D_FF], SWIGLU_LIMIT)
        lin = jnp.clip(hh[:, D_FF:2 * D_FF], -SWIGLU_LIMIT, SWIGLU_LIMIT)
        act = glu * _sigmoid(SWIGLU_ALPHA * glu) * (lin + 1.0)
        y = _dot(act.astype(BF16), w2_s[...]) + b2_ref[0]
        for c in range(ROW_SPLIT):
            o_ref[pl.ds(c, tm, stride=ROW_SPLIT), :] = y[:, c * LANES:(c + 1) * LANES]

        @pl.when(i == last)
        def _():
            _wait_rows(tm, h_hbm, nxt, nxt_sem)

    @pl.when(lax.rem(i, 2) == 0)
    def _():
        step(xa, sem.at[0], xb, sem.at[1])

    @pl.when(lax.rem(i, 2) == 1)
    def _():
        step(xb, sem.at[1], xa, sem.at[0])


def _expert_call(tile_expert, src_rows, h2, w1, b1, w2, b2, layer):
    ntiles = tile_expert.shape[0]
    tm = EXP_TILE
    src3 = src_rows.reshape(ntiles, 1, tm)
    perm = np.zeros((2 * LANES, 2 * LANES), np.float32)
    perm[2 * np.arange(LANES), np.arange(LANES)] = 1.0
    perm[2 * np.arange(LANES) + 1, LANES + np.arange(LANES)] = 1.0
    grid_spec = pltpu.PrefetchScalarGridSpec(
        num_scalar_prefetch=1,
        grid=(ntiles,),
        in_specs=[pl.BlockSpec((1, 1, tm), lambda i, te: (i, 0, 0), memory_space=pltpu.SMEM),
                  pl.BlockSpec((1, 1, tm), lambda i, te: (jnp.minimum(i + 1, ntiles - 1), 0, 0),
                               memory_space=pltpu.SMEM),
                  pl.BlockSpec(memory_space=pl.ANY),
                  pl.BlockSpec((1, 1, D_MODEL, 2 * D_FF), lambda i, te: (layer, te[i], 0, 0)),
                  pl.BlockSpec((2 * LANES, 2 * LANES), lambda i, te: (0, 0)),
                  pl.BlockSpec((1, 1, 2 * D_FF), lambda i, te: (te[i], 0, 0)),
                  pl.BlockSpec((1, 1, D_FF, D_MODEL), lambda i, te: (layer, te[i], 0, 0)),
                  pl.BlockSpec((1, 1, D_MODEL), lambda i, te: (te[i], 0, 0))],
        out_specs=pl.BlockSpec((tm * ROW_SPLIT, LANES), lambda i, te: (i, 0)),
        scratch_shapes=[pltpu.VMEM((tm * ROW_SPLIT, LANES), F32),
                        pltpu.VMEM((tm * ROW_SPLIT, LANES), F32),
                        pltpu.VMEM((D_MODEL, 2 * D_FF), BF16),
                        pltpu.VMEM((D_FF, D_MODEL), BF16),
                        pltpu.SemaphoreType.DMA((2,))],
    )
    return pl.pallas_call(
        _expert_kernel,
        grid_spec=grid_spec,
        out_shape=jax.ShapeDtypeStruct((ntiles * tm * ROW_SPLIT, LANES), F32),
        compiler_params=_cparams(("arbitrary",)),
    )(tile_expert, src3, src3, h2, w1, jnp.asarray(perm, BF16), b1, w2, b2)


def _combine_kernel(pos_ref, nxt_ref, y_hbm, wt_ref, x_ref, pg_ref, g_ref, o_ref, ya, yb, sem):
    i = pl.program_id(0) * pl.num_programs(1) + pl.program_id(1)
    last = pl.num_programs(0) * pl.num_programs(1) - 1
    tm = x_ref.shape[0]

    def start(idx_ref, buf, s):
        for k in range(TOP_K):
            _start_rows(idx_ref, tm, y_hbm, buf.at[k], s, idx_off=k * tm)

    def wait(buf, s):
        for k in range(TOP_K):
            _wait_rows(tm, y_hbm, buf.at[k], s)

    @pl.when(i == 0)
    def _():
        start(pos_ref, ya, sem.at[0])

    def step(cur, cur_sem, nxt, nxt_sem):
        wait(cur, cur_sem)
        start(nxt_ref, nxt, nxt_sem)
        wt = wt_ref[...]
        y = wt[:, 0:1] * _tile_rows(cur.at[0], tm)
        for k in range(1, TOP_K):
            y = y + wt[:, k:k + 1] * _tile_rows(cur.at[k], tm)
        ms = jnp.mean(y * y, axis=-1, keepdims=True)
        o_ref[...] = x_ref[...] + g_ref[0] * (y * lax.rsqrt(ms + EPS) * pg_ref[...])

        @pl.when(i == last)
        def _():
            wait(nxt, nxt_sem)

    @pl.when(lax.rem(i, 2) == 0)
    def _():
        step(ya, sem.at[0], yb, sem.at[1])

    @pl.when(lax.rem(i, 2) == 1)
    def _():
        step(yb, sem.at[1], ya, sem.at[0])


def _combine_call(dest, y_sorted, wts, x1, post_g, g2, batch, seq):
    t = x1.shape[0]
    tm = min(CMB_TILE, seq)
    nt = seq // tm
    ntiles = t // tm
    dest_t = dest.reshape(ntiles, tm, TOP_K).transpose(0, 2, 1).reshape(ntiles, 1, TOP_K * tm)
    row = lambda b, i: (b * nt + i, 0)
    return pl.pallas_call(
        _combine_kernel,
        grid=(batch, nt),
        in_specs=[pl.BlockSpec((1, 1, TOP_K * tm), lambda b, i: (b * nt + i, 0, 0),
                               memory_space=pltpu.SMEM),
                  pl.BlockSpec((1, 1, TOP_K * tm),
                               lambda b, i: (jnp.minimum(b * nt + i + 1, ntiles - 1), 0, 0),
                               memory_space=pltpu.SMEM),
                  pl.BlockSpec(memory_space=pl.ANY),
                  pl.BlockSpec((tm, LANES), row),
                  pl.BlockSpec((tm, D_MODEL), row),
                  pl.BlockSpec((1, D_MODEL), lambda b, i: (0, 0)),
                  pl.BlockSpec((1, 1, D_MODEL), lambda b, i: (b, 0, 0))],
        out_specs=pl.BlockSpec((tm, D_MODEL), row),
        out_shape=jax.ShapeDtypeStruct((t, D_MODEL), F32),
        scratch_shapes=[pltpu.VMEM((TOP_K, tm * ROW_SPLIT, LANES), F32),
                        pltpu.VMEM((TOP_K, tm * ROW_SPLIT, LANES), F32),
                        pltpu.SemaphoreType.DMA((2,))],
        compiler_params=_cparams(("arbitrary", "arbitrary")),
    )(dest_t, dest_t, y_sorted, wts, x1, post_g, g2)


def _dispatch_plan(idx, rank, counts, t):
    tm = EXP_TILE
    npairs = t * TOP_K
    ntiles = npairs // tm + N_EXPERTS
    e_flat = idx.reshape(-1)
    pair = jnp.arange(npairs, dtype=jnp.int32)
    _, order = lax.sort((e_flat, pair), num_keys=1, is_stable=True)
    cnt_excl = jnp.cumsum(counts) - counts
    tiles_per = (counts + tm - 1) // tm
    tile_end = jnp.cumsum(tiles_per)
    tile_start = tile_end - tiles_per
    tile_ids = jnp.arange(ntiles, dtype=jnp.int32)
    tile_expert = jnp.minimum(
        jnp.sum((tile_ids[:, None] >= tile_end[None, :]).astype(jnp.int32), axis=1),
        N_EXPERTS - 1).astype(jnp.int32)
    rows = jnp.arange(ntiles * tm, dtype=jnp.int32)
    row_e = jnp.repeat(tile_expert, tm)
    off = rows - tile_start[row_e] * tm
    valid = (off < counts[row_e]) & (rows < tile_end[-1] * tm)
    sidx = jnp.clip(cnt_excl[row_e] + off, 0, npairs - 1)
    src_rows = jnp.where(valid, order[sidx] // TOP_K, 0).astype(jnp.int32)
    dest = (tile_start[idx] * tm + rank).astype(jnp.int32)
    return tile_expert, src_rows, dest


def _rope_tables():
    lane = np.arange(GROUP)
    d = lane % DIFF_D
    rd = DIFF_D // 4
    half = rd // 2
    inv = np.where(d < rd, ROPE_THETA ** (-(d % half).astype(np.float32) / half), 0.0)
    ma = np.where(d < half, -1.0, 0.0)
    mb = np.where((d >= half) & (d < rd), 1.0, 0.0)
    f = lambda a: jnp.asarray(a.reshape(1, GROUP), F32)
    return f(inv), f(ma), f(mb)


def kernel(x, c, positions, ada_w, ada_b, pre_norm_g, post_norm_g, w_in, w_out, fox_fb, hg_lower, hg_norm_g, diff_lam_q1, diff_lam_k1, diff_lam_q2, diff_lam_k2, diff_subln_g, s5_a_re, s5_a_im, s5_log_step, s5_b_re, s5_b_im, s5_c_re, s5_c_im, s5_d, s5_glu_w, router_w, router_b, exp_w1, exp_b1, exp_w2, exp_b2):
    batch, seq, d = x.shape
    depth = ada_w.shape[0]
    t = batch * seq
    assert d == D_MODEL and seq % ROW_TILE == 0

    lb_all = jnp.cumsum(jax.nn.softmax(hg_lower.astype(F32), axis=0), axis=0)
    lb_all = lb_all - lb_all[0:1]
    mod = _ada_call(c, ada_w, ada_b)
    inv_lane, mask_a, mask_b = _rope_tables()
    pos = positions.reshape(t, 1).astype(jnp.int32)
    xf = x.reshape(t, d)

    w_main, w_f = _win_prep_call(w_in)
    for l in range(depth):
        m6 = mod[l].reshape(batch, 6, 1, d)
        sh1, sc1, g1, sh2, sc2, g2 = [m6[:, j] for j in range(6)]
        fb = jnp.pad(fox_fb[l], (0, LANES - FOX_HEADS)).reshape(1, LANES)
        fox, frow, hg, diff, s5u = _inproj_call(
            xf, pre_norm_g[l, 0].reshape(1, d), sc1, sh1, w_main, w_f, fb, pos,
            inv_lane, mask_a, mask_b, batch, seq, l)

        y_fox = _fox_call(fox, frow, batch, seq)

        lam_init = 0.8 - 0.6 * math.exp(-0.3 * l)
        lam = (jnp.exp(jnp.sum(diff_lam_q1[l].astype(F32) * diff_lam_k1[l].astype(F32)))
               - jnp.exp(jnp.sum(diff_lam_q2[l].astype(F32) * diff_lam_k2[l].astype(F32))) + lam_init)
        y_diff = _diff_call(diff, lam.reshape(1), diff_subln_g[l].reshape(1, DIFF_DV),
                            1.0 - lam_init, batch, seq)

        y_hg = _hgrn2_call(hg, lb_all[l].reshape(1, -1), hg_norm_g[l].reshape(1, HG_DV), batch, seq)

        bbar, ar, ai, cmat = _s5_params(s5_a_re[l], s5_a_im[l], s5_log_step[l], s5_b_re[l],
                                        s5_b_im[l], s5_c_re[l], s5_c_im[l])
        u_sb = s5u.reshape(batch, seq, GROUP).transpose(1, 0, 2).reshape(t, GROUP)
        y_s5_sb = _s5_call(u_sb, bbar, ar, ai, cmat, s5_d[l].reshape(1, GROUP),
                           s5_glu_w[l].astype(BF16), batch, seq)
        y_s5 = y_s5_sb.reshape(seq, batch, GROUP).transpose(1, 0, 2).reshape(t, GROUP)

        rw = jnp.pad(router_w[l], ((0, 0), (0, LANES - N_EXPERTS)))
        rb = jnp.pad(router_b[l], (0, LANES - N_EXPERTS), constant_values=NEG_BIG).reshape(1, LANES)
        x1, h2, idx, wts, cnt = _outproj_call(
            y_fox, y_hg, y_diff, y_s5, w_out[l].astype(BF16), xf,
            post_norm_g[l, 0].reshape(1, d), g1, pre_norm_g[l, 1].reshape(1, d), sc2, sh2,
            rw, rb, batch, seq)

        tile_expert, src_rows, dest = _dispatch_plan(
            idx[:, 0:TOP_K], idx[:, TOP_K:2 * TOP_K], cnt[0, 0:N_EXPERTS].astype(jnp.int32), t)
        b1 = exp_b1[l]
        b1p = jnp.concatenate([b1[:, 0::2], b1[:, 1::2]], axis=-1).reshape(N_EXPERTS, 1, 2 * D_FF)
        y_sorted = _expert_call(tile_expert, src_rows, h2, exp_w1, b1p, exp_w2,
                                exp_b2[l].reshape(N_EXPERTS, 1, D_MODEL), l)
        xf = _combine_call(dest, y_sorted, wts, x1, post_norm_g[l, 1].reshape(1, d), g2, batch, seq)

    return xf.reshape(batch, seq, d)
```

```python
import functools
import math

import numpy as np
import jax
import jax.numpy as jnp
from jax import lax
from jax.experimental import pallas as pl
from jax.experimental.pallas import tpu as pltpu

F32 = jnp.float32
BF16 = jnp.bfloat16

D_MODEL = 1024
GROUP = 256
FOX_HEADS = 4
FOX_HD = 64
HG_HEADS = 4
HG_DK = 128
HG_DV = 64
DIFF_HEADS = 4
DIFF_DV = 64
DIFF_D = 32
S5_CH = 16
S5_GROUPS = 16
S5_N = 64
ROPE_THETA = 500000.0
N_EXPERTS = 32
TOP_K = 4
D_FF = 1024
SWIGLU_LIMIT = 7.0
SWIGLU_ALPHA = 1.702
EPS = 1e-6
NEG_BIG = -1e30

LANES = 128
VMEM_LIMIT = 56 * 1024 * 1024

LOG2E = 1.4426950408889634

ROW_TILE = 512
ATT_T = ROW_TILE
HG_CHUNK = 64
S5_STEPS = 128
EXP_TILE = 256
CMB_TILE = 256

ROW_SPLIT = D_MODEL // LANES

HG_COLS = 2 * HG_HEADS * HG_DK + 2 * GROUP
MAIN_COLS = 3 * GROUP + HG_COLS + 3 * GROUP + GROUP


def _cparams(sem):
    return pltpu.CompilerParams(dimension_semantics=sem, vmem_limit_bytes=VMEM_LIMIT)


def _sigmoid(x):
    return 1.0 / (1.0 + jnp.exp(-x))


def _shr(x, pow2):
    return lax.shift_right_logical(x, int(math.log2(pow2)))


def _dot(a, b):
    return jnp.dot(a, b, preferred_element_type=F32)


def _dot_nt(a, b):
    return lax.dot_general(a, b, (((1,), (1,)), ((), ())), preferred_element_type=F32)


def _dot_tn(a, b):
    return lax.dot_general(a, b, (((0,), (0,)), ((), ())), preferred_element_type=F32)


def _split_dot(m_bf16, x, terms=3):
    out = None
    for _ in range(terms):
        piece = x.astype(BF16)
        x = x - piece.astype(F32)
        d = _dot(m_bf16, piece)
        out = d if out is None else out + d
    return out


def _ada_kernel(c_ref, w_ref, b_ref, o_ref):
    c = c_ref[...]
    cs = c * _sigmoid(c)
    o_ref[0] = jnp.dot(cs, w_ref[0], preferred_element_type=F32,
                       precision=lax.Precision.HIGHEST) + b_ref[0]


def _ada_call(c, ada_w, ada_b):
    depth, d, n = ada_w.shape
    b = c.shape[0]
    tn = 1536
    return pl.pallas_call(
        _ada_kernel,
        grid=(depth, n // tn),
        in_specs=[pl.BlockSpec((b, d), lambda l, j: (0, 0)),
                  pl.BlockSpec((1, d, tn), lambda l, j: (l, 0, j)),
                  pl.BlockSpec((1, 1, tn), lambda l, j: (l, 0, j))],
        out_specs=pl.BlockSpec((1, b, tn), lambda l, j: (l, 0, j)),
        out_shape=jax.ShapeDtypeStruct((depth, b, n), F32),
        compiler_params=_cparams(("arbitrary", "arbitrary")),
    )(c, ada_w, ada_b.reshape(depth, 1, n))


_FOX_COLS = 3 * GROUP


def _win_prep_kernel(w_ref, main_ref, f_ref):
    rows = w_ref.shape[1]
    lane = lax.broadcasted_iota(jnp.int32, (rows, LANES), 1)
    main_ref[0, :, 0:_FOX_COLS] = w_ref[0, :, 0:_FOX_COLS].astype(BF16)
    first = w_ref[0, :, _FOX_COLS:_FOX_COLS + LANES]
    f_ref[0] = jnp.where(lane < FOX_HEADS, first, 0.0).astype(BF16)
    cur = pltpu.roll(first, LANES - FOX_HEADS, axis=1)
    for m in range((MAIN_COLS - _FOX_COLS) // LANES):
        c0 = _FOX_COLS + m * LANES
        nxt = pltpu.roll(w_ref[0, :, c0 + LANES:c0 + 2 * LANES], LANES - FOX_HEADS, axis=1)
        main_ref[0, :, c0:c0 + LANES] = jnp.where(lane < LANES - FOX_HEADS, cur, nxt).astype(BF16)
        cur = nxt


def _win_prep_call(w_in):
    depth, d, cols = w_in.shape
    padded = MAIN_COLS + LANES
    w_pad = jnp.pad(w_in, ((0, 0), (0, 0), (0, padded - cols)))
    rows = 256
    return pl.pallas_call(
        _win_prep_kernel,
        grid=(depth, d // rows),
        in_specs=[pl.BlockSpec((1, rows, padded), lambda l, r: (l, r, 0))],
        out_specs=[pl.BlockSpec((1, rows, MAIN_COLS), lambda l, r: (l, r, 0)),
                   pl.BlockSpec((1, rows, LANES), lambda l, r: (l, r, 0))],
        out_shape=[jax.ShapeDtypeStruct((depth, d, MAIN_COLS), BF16),
                   jax.ShapeDtypeStruct((depth, d, LANES), BF16)],
        compiler_params=_cparams(("arbitrary", "arbitrary")),
    )(w_pad)


def _inproj_kernel(x_ref, g_ref, sc_ref, sh_ref, w_ref, wf_ref, fb_ref, pos_ref, inv_ref,
                   ma_ref, mb_ref,
                   fox_ref, frow_ref, hg_ref, diff_ref, s5_ref, carry_ref):
    i = pl.program_id(1)
    tm = x_ref.shape[0]

    @pl.when(i == 0)
    def _():
        carry_ref[...] = jnp.zeros_like(carry_ref)

    x = x_ref[...]
    ms = jnp.mean(x * x, axis=-1, keepdims=True)
    h = (x * lax.rsqrt(ms + EPS) * g_ref[...]) * (1.0 + sc_ref[0]) + sh_ref[0]
    hb = h.astype(BF16)

    o = 0
    pf = _dot(hb, w_ref[0, :, o:o + 3 * GROUP])
    fox_ref[:, 0:GROUP] = (pf[:, 0:GROUP] * (FOX_HD ** -0.5 * LOG2E)).astype(BF16)
    fox_ref[:, GROUP:3 * GROUP] = pf[:, GROUP:3 * GROUP].astype(BF16)
    o += 3 * GROUP

    ff = _dot(hb, wf_ref[0]) + fb_ref[...]
    logf = jnp.minimum(ff, 0.0) - jnp.log(1.0 + jnp.exp(-jnp.abs(ff)))
    r = lax.broadcasted_iota(jnp.int32, (tm, tm), 0)
    cidx = lax.broadcasted_iota(jnp.int32, (tm, tm), 1)
    tri = jnp.where(cidx <= r, 1.0, 0.0).astype(BF16)
    cum = _split_dot(tri, logf) + carry_ref[...]
    carry_ref[...] = cum[tm - 1:tm, :]
    cum_t = jnp.transpose(cum * LOG2E)
    frow_ref[0, 0] = cum_t[0:8, :]

    hg_ref[...] = _dot(hb, w_ref[0, :, o:o + HG_COLS]).astype(BF16)
    o += HG_COLS

    pd = _dot(hb, w_ref[0, :, o:o + 3 * GROUP])
    o += 3 * GROUP
    pos = pos_ref[...].astype(F32)
    ang = pos * inv_ref[...]
    cos = jnp.cos(ang)
    sin = jnp.sin(ang)
    ma = ma_ref[...]
    mb = mb_ref[...]

    def rope(xx):
        outs = []
        for j in range(GROUP // LANES):
            sl = slice(j * LANES, (j + 1) * LANES)
            xs = xx[:, sl]
            partner = (pltpu.roll(xs, LANES - 4, axis=1) * ma[:, sl]
                       + pltpu.roll(xs, 4, axis=1) * mb[:, sl])
            outs.append(xs * cos[:, sl] + partner * sin[:, sl])
        return jnp.concatenate(outs, axis=1)

    diff_ref[:, 0:GROUP] = (rope(pd[:, 0:GROUP]) * (DIFF_D ** -0.5 * LOG2E)).astype(BF16)
    diff_ref[:, GROUP:2 * GROUP] = rope(pd[:, GROUP:2 * GROUP]).astype(BF16)
    diff_ref[:, 2 * GROUP:3 * GROUP] = pd[:, 2 * GROUP:3 * GROUP].astype(BF16)

    s5_ref[...] = _dot(hb, w_ref[0, :, o:o + GROUP]).astype(BF16)


def _inproj_call(x2, g_pre, sc, sh, w_main, w_f, fb, pos, inv_lane, mask_a, mask_b, batch, seq,
                 layer):
    t = x2.shape[0]
    tm = ROW_TILE
    nt = seq // tm
    row = lambda b, i: (b * nt + i, 0)
    const2 = lambda b, i: (0, 0)
    per_b = lambda b, i: (b, 0, 0)
    outs = pl.pallas_call(
        _inproj_kernel,
        grid=(batch, nt),
        in_specs=[pl.BlockSpec((tm, D_MODEL), row),
                  pl.BlockSpec((1, D_MODEL), const2),
                  pl.BlockSpec((1, 1, D_MODEL), per_b),
                  pl.BlockSpec((1, 1, D_MODEL), per_b),
                  pl.BlockSpec((1, D_MODEL, MAIN_COLS), lambda b, i: (layer, 0, 0)),
                  pl.BlockSpec((1, D_MODEL, LANES), lambda b, i: (layer, 0, 0)),
                  pl.BlockSpec((1, LANES), const2),
                  pl.BlockSpec((tm, 1), row),
                  pl.BlockSpec((1, GROUP), const2),
                  pl.BlockSpec((1, GROUP), const2),
                  pl.BlockSpec((1, GROUP), const2)],
        out_specs=[pl.BlockSpec((tm, 3 * GROUP), row),
                   pl.BlockSpec((1, 1, 8, tm), lambda b, i: (b, i, 0, 0)),
                   pl.BlockSpec((tm, HG_COLS), row),
                   pl.BlockSpec((tm, 3 * GROUP), row),
                   pl.BlockSpec((tm, GROUP), row)],
        out_shape=[jax.ShapeDtypeStruct((t, 3 * GROUP), BF16),
                   jax.ShapeDtypeStruct((batch, nt, 8, tm), F32),
                   jax.ShapeDtypeStruct((t, HG_COLS), BF16),
                   jax.ShapeDtypeStruct((t, 3 * GROUP), BF16),
                   jax.ShapeDtypeStruct((t, GROUP), BF16)],
        scratch_shapes=[pltpu.VMEM((1, LANES), F32)],
        compiler_params=_cparams(("arbitrary", "arbitrary")),
    )(x2, g_pre, sc, sh, w_main, w_f, fb, pos, inv_lane, mask_a, mask_b)
    return outs


_HEAD_W = 64
_HEADS = 4


def _slot(x64):
    return jnp.concatenate([x64, jnp.zeros_like(x64)], axis=1)


def _pad_kv(k_ref, v_ref, k_s, v_s):
    seq = k_ref.shape[0]
    rows = ATT_T
    lane = lax.broadcasted_iota(jnp.int32, (rows, LANES), 1)

    def body(c, _):
        r0 = pl.multiple_of(c * rows, rows)
        for h in range(_HEADS):
            k_s[pl.ds(r0, rows), h * LANES:(h + 1) * LANES] = _slot(
                k_ref[pl.ds(r0, rows), h * _HEAD_W:(h + 1) * _HEAD_W])
            v = _slot(v_ref[pl.ds(r0, rows), h * _HEAD_W:(h + 1) * _HEAD_W])
            v_s[pl.ds(r0, rows), h * LANES:(h + 1) * LANES] = jnp.where(
                lane == _HEAD_W, jnp.ones_like(v), v)
        return 0

    lax.fori_loop(0, seq // rows, body, 0)


def _attend(qs, slots, scratch, qi, key_bias):
    k_s, v_s = scratch
    n = len(qs)
    tq = qs[0].shape[0]
    tk = ATT_T

    def block(kb, carry, keep):
        ms, accs = carry
        start = pl.multiple_of(kb * tk, tk)
        new_m, new_acc = [], []
        for j in range(n):
            c0 = slots[j] * LANES
            s = _dot_nt(qs[j], k_s[pl.ds(start, tk), c0:c0 + LANES])
            kbias = key_bias(j, kb)
            if kbias is not None:
                s = s - kbias
            if keep is not None:
                s = jnp.where(keep, s, NEG_BIG)
            m_new = jnp.maximum(ms[j], jnp.max(s, axis=-1, keepdims=True))
            alpha = jnp.exp2(ms[j] - m_new)
            p = jnp.exp2(s - m_new).astype(BF16)
            new_acc.append(alpha * accs[j] + _dot(p, v_s[pl.ds(start, tk), c0:c0 + LANES]))
            new_m.append(m_new)
        return tuple(new_m), tuple(new_acc)

    init = (tuple(jnp.full((tq, 1), NEG_BIG, F32) for _ in range(n)),
            tuple(jnp.zeros((tq, LANES), F32) for _ in range(n)))
    carry = lax.fori_loop(0, qi, lambda kb, c: block(kb, c, None), init)
    causal = (lax.broadcasted_iota(jnp.int32, (tq, tk), 1)
              <= lax.broadcasted_iota(jnp.int32, (tq, tk), 0))
    return block(qi, carry, causal)[1]


def _fox_kernel(q_ref, k_ref, v_ref, frow_ref, o_ref, k_s, v_s):
    qi = pl.program_id(1)

    @pl.when(qi == 0)
    def _():
        _pad_kv(k_ref, v_ref, k_s, v_s)

    qs = [_slot(q_ref[:, h * _HEAD_W:(h + 1) * _HEAD_W]) for h in range(_HEADS)]
    accs = _attend(qs, list(range(_HEADS)), (k_s, v_s), qi,
                   lambda j, kb: frow_ref[0, kb, j:j + 1, :])
    outs = [a[:, 0:_HEAD_W] / a[:, _HEAD_W:_HEAD_W + 1] for a in accs]
    o_ref[...] = jnp.concatenate(outs, axis=1).astype(BF16)


def _att_specs(seq):
    nq = seq // ATT_T
    return ([pl.BlockSpec((ATT_T, GROUP), lambda b, i: (b * nq + i, 0)),
             pl.BlockSpec((seq, GROUP), lambda b, i: (b, 1)),
             pl.BlockSpec((seq, GROUP), lambda b, i: (b, 2))],
            pl.BlockSpec((ATT_T, GROUP), lambda b, i: (b * nq + i, 0)),
            [pltpu.VMEM((seq, _HEADS * LANES), BF16), pltpu.VMEM((seq, _HEADS * LANES), BF16)])


def _fox_call(fox, frow, batch, seq):
    t = fox.shape[0]
    nq = seq // ATT_T
    qkv_specs, out_spec, scratch = _att_specs(seq)
    return pl.pallas_call(
        _fox_kernel,
        grid=(batch, nq),
        in_specs=qkv_specs + [pl.BlockSpec((1, nq, 8, ATT_T), lambda b, i: (b, 0, 0, 0))],
        out_specs=out_spec,
        out_shape=jax.ShapeDtypeStruct((t, GROUP), BF16),
        scratch_shapes=scratch,
        compiler_params=_cparams(("arbitrary", "arbitrary")),
    )(fox, fox, fox, frow)


def _diff_kernel(lam_ref, q_ref, k_ref, v_ref, g_ref, o_ref, k_s, v_s, *, out_scale):
    qi = pl.program_id(1)

    @pl.when(qi == 0)
    def _():
        _pad_kv(k_ref, v_ref, k_s, v_s)

    lam = lam_ref[0]
    lane = lax.broadcasted_iota(jnp.int32, (ATT_T, LANES), 1)
    outs = []
    for pair in range(_HEADS // 2):
        qs, slots = [], []
        for h in (2 * pair, 2 * pair + 1):
            q = _slot(q_ref[:, h * _HEAD_W:(h + 1) * _HEAD_W])
            qs += [jnp.where(lane < DIFF_D, q, jnp.zeros_like(q)),
                   jnp.where(lane >= DIFF_D, q, jnp.zeros_like(q))]
            slots += [h, h]
        accs = _attend(qs, slots, (k_s, v_s), qi, lambda j, kb: None)
        for j in (0, 2):
            y = (accs[j][:, 0:_HEAD_W] / accs[j][:, _HEAD_W:_HEAD_W + 1]
                 - lam * (accs[j + 1][:, 0:_HEAD_W] / accs[j + 1][:, _HEAD_W:_HEAD_W + 1]))
            ms = jnp.mean(y * y, axis=-1, keepdims=True)
            outs.append(y * lax.rsqrt(ms + EPS) * g_ref[...] * out_scale)
    o_ref[...] = jnp.concatenate(outs, axis=1).astype(BF16)


def _diff_call(diff, lam, subln_g, out_scale, batch, seq):
    t = diff.shape[0]
    nq = seq // ATT_T
    qkv_specs, out_spec, scratch = _att_specs(seq)
    return pl.pallas_call(
        functools.partial(_diff_kernel, out_scale=out_scale),
        grid=(batch, nq),
        in_specs=([pl.BlockSpec(memory_space=pltpu.SMEM)] + qkv_specs
                  + [pl.BlockSpec((1, DIFF_DV), lambda b, i: (0, 0))]),
        out_specs=out_spec,
        out_shape=jax.ShapeDtypeStruct((t, GROUP), BF16),
        scratch_shapes=scratch,
        compiler_params=_cparams(("arbitrary", "arbitrary")),
    )(lam, diff, diff, diff, subln_g)


_HG_LEVELS = (32, 16, 8)


def _hg_ref_rows(G, level):
    C = G.shape[0]
    rows = []
    for tile in range(C // 8):
        if level is None:
            r = tile * 8 + 3
        else:
            grp = (tile * 8) // (2 * level)
            r = grp * 2 * level + level - 1
        rows.append(jnp.broadcast_to(G[r:r + 1, :], (8, G.shape[1])))
    return jnp.concatenate(rows, axis=0)


def _hgrn2_kernel(p_ref, lb_ref, ng_ref, o_ref, state_ref):
    i = pl.program_id(1)
    tm = p_ref.shape[0]
    C = HG_CHUNK
    KW = HG_HEADS * HG_DK

    @pl.when(i == 0)
    def _():
        state_ref[...] = jnp.zeros_like(state_ref)

    lb = lb_ref[...]
    rr = lax.broadcasted_iota(jnp.int32, (C, C), 0)
    cc = lax.broadcasted_iota(jnp.int32, (C, C), 1)
    tri = jnp.where(cc <= rr, 1.0, 0.0).astype(BF16)
    level_masks = []
    for m in _HG_LEVELS:
        same_grp = _shr(rr, 2 * m) == _shr(cc, 2 * m)
        level_masks.append(same_grp & ((rr & (2 * m - 1)) >= m) & ((cc & (2 * m - 1)) < m))
    diag_mask = (_shr(rr, 8) == _shr(cc, 8)) & (cc <= rr)
    row_idx = lax.broadcasted_iota(jnp.int32, (C, HG_DK), 0)

    def chunk(base, states):
        hq = p_ref[pl.ds(base, C), 0:KW].astype(F32)
        q_all = hq * _sigmoid(hq)
        f = lb + (1.0 - lb) * _sigmoid(p_ref[pl.ds(base, C), KW:2 * KW].astype(F32))
        k_all = 1.0 - f
        g_all = _split_dot(tri, jnp.log(f), terms=2)
        new_states = []
        for h in range(HG_HEADS):
            ks = slice(h * HG_DK, (h + 1) * HG_DK)
            q = q_all[:, ks]
            k = k_all[:, ks]
            G = g_all[:, ks]
            v = p_ref[pl.ds(base, C), 2 * KW + h * HG_DV:2 * KW + (h + 1) * HG_DV]
            st = states[h]
            gl = G[C - 1:C, :]
            o = _dot_nt((q * jnp.exp(G)).astype(BF16), st.astype(BF16))
            a = jnp.zeros((C, C), F32)
            for m, mask in zip(_HG_LEVELS, level_masks):
                ref = _hg_ref_rows(G, m)
                upper = (row_idx & (2 * m - 1)) >= m
                e = jnp.exp(jnp.where(upper, G - ref, ref - G))
                am = _dot_nt((q * e).astype(BF16), (k * e).astype(BF16))
                a = a + jnp.where(mask, am, 0.0)
            ref = _hg_ref_rows(G, None)
            d = jnp.clip(G - ref, -80.0, 80.0)
            am = _dot_nt((q * jnp.exp(d)).astype(BF16), (k * jnp.exp(-d)).astype(BF16))
            a = a + jnp.where(diag_mask, am, 0.0)
            o = o + _dot(a.astype(BF16), v)
            kd = (k * jnp.exp(gl - G)).astype(BF16)
            new_states.append(jnp.exp(gl) * st + _dot_tn(v, kd))
            ms = jnp.mean(o * o, axis=-1, keepdims=True)
            on = o * lax.rsqrt(ms + EPS) * ng_ref[...]
            gate = p_ref[pl.ds(base, C), 2 * KW + GROUP + h * HG_DV:
                         2 * KW + GROUP + (h + 1) * HG_DV].astype(F32)
            o_ref[pl.ds(base, C), h * HG_DV:(h + 1) * HG_DV] = (
                on * (gate * _sigmoid(gate))).astype(BF16)
        return tuple(new_states)

    def pair(ci, states):
        first = pl.multiple_of(ci * 2 * C, 2 * C)
        second = pl.multiple_of(ci * 2 * C + C, C)
        return chunk(second, chunk(first, states))

    states = lax.fori_loop(0, tm // (2 * C), pair,
                           tuple(state_ref[h] for h in range(HG_HEADS)))
    for h in range(HG_HEADS):
        state_ref[h] = states[h]


def _hgrn2_call(hg, lb, norm_g, batch, seq):
    t = hg.shape[0]
    tm = min(ROW_TILE, seq)
    nt = seq // tm
    kw = HG_HEADS * HG_DK
    return pl.pallas_call(
        _hgrn2_kernel,
        grid=(batch, nt),
        in_specs=[pl.BlockSpec((tm, HG_COLS), lambda b, i: (b * nt + i, 0)),
                  pl.BlockSpec((1, kw), lambda b, i: (0, 0)),
                  pl.BlockSpec((1, HG_DV), lambda b, i: (0, 0))],
        out_specs=pl.BlockSpec((tm, GROUP), lambda b, i: (b * nt + i, 0)),
        out_shape=jax.ShapeDtypeStruct((t, GROUP), BF16),
        scratch_shapes=[pltpu.VMEM((HG_HEADS, HG_DV, HG_DK), F32)],
        compiler_params=_cparams(("arbitrary", "arbitrary")),
    )(hg, lb, norm_g)


def _s5_kernel(u_ref, bb_ref, ar_ref, ai_ref, cm_ref, d_ref, glu_ref, o_ref, bu_s, xr_s, xi_s,
               *, batch):
    i = pl.program_id(0)
    nst = S5_GROUPS * S5_N
    steps = u_ref.shape[0] // batch

    @pl.when(i == 0)
    def _():
        xr_s[...] = jnp.zeros_like(xr_s)
        xi_s[...] = jnp.zeros_like(xi_s)

    u = u_ref[...]
    bu_s[...] = _dot(u, bb_ref[...])
    ar = jnp.broadcast_to(ar_ref[...], (batch, nst))
    ai = jnp.broadcast_to(ai_ref[...], (batch, nst))

    def step(t, carry):
        xr, xi = carry
        r0 = pl.multiple_of(t * batch, batch)
        nr = ar * xr - ai * xi + bu_s[pl.ds(r0, batch), 0:nst]
        ni = ar * xi + ai * xr + bu_s[pl.ds(r0, batch), nst:2 * nst]
        bu_s[pl.ds(r0, batch), 0:nst] = nr
        bu_s[pl.ds(r0, batch), nst:2 * nst] = ni
        return nr, ni

    xr, xi = lax.fori_loop(0, steps, step, (xr_s[...], xi_s[...]))
    xr_s[...] = xr
    xi_s[...] = xi
    y = _dot(bu_s[...].astype(BF16), cm_ref[...]) + d_ref[...] * u.astype(F32)
    gel = 0.5 * y * (1.0 + jnp.tanh(0.7978845608028654 * (y + 0.044715 * (y * y * y))))
    z = _dot(gel.astype(BF16), glu_ref[...])
    o_ref[...] = (z[:, 0:GROUP] * _sigmoid(z[:, GROUP:2 * GROUP])).astype(BF16)


def _s5_call(u_sb, bbar, ar, ai, cmat, dskip, glu_w, batch, seq):
    rows = u_sb.shape[0]
    steps = min(S5_STEPS, seq)
    tr = steps * batch
    nst = S5_GROUPS * S5_N
    const = lambda i: (0, 0)
    return pl.pallas_call(
        functools.partial(_s5_kernel, batch=batch),
        grid=(seq // steps,),
        in_specs=[pl.BlockSpec((tr, GROUP), lambda i: (i, 0)),
                  pl.BlockSpec((GROUP, 2 * nst), const),
                  pl.BlockSpec((1, nst), const),
                  pl.BlockSpec((1, nst), const),
                  pl.BlockSpec((2 * nst, GROUP), const),
                  pl.BlockSpec((1, GROUP), const),
                  pl.BlockSpec((GROUP, 2 * GROUP), const)],
        out_specs=pl.BlockSpec((tr, GROUP), lambda i: (i, 0)),
        out_shape=jax.ShapeDtypeStruct((rows, GROUP), BF16),
        scratch_shapes=[pltpu.VMEM((tr, 2 * nst), F32),
                        pltpu.VMEM((batch, nst), F32), pltpu.VMEM((batch, nst), F32)],
        compiler_params=_cparams(("arbitrary",)),
    )(u_sb, bbar, ar, ai, cmat, dskip, glu_w)


def _s5_params(a_re, a_im, log_step, b_re, b_im, c_re, c_im):
    lr = jnp.minimum(a_re.astype(F32), -1e-4)
    li = a_im.astype(F32)
    dt = jnp.exp(log_step.astype(F32))[:, None]
    mag = jnp.exp(lr * dt)
    ar = mag * jnp.cos(li * dt)
    ai = mag * jnp.sin(li * dt)
    den = lr * lr + li * li
    zr = ((ar - 1.0) * lr + ai * li) / den
    zi = (ai * lr - (ar - 1.0) * li) / den
    br = b_re.astype(F32)
    bi = b_im.astype(F32)
    bbr = zr[..., None] * br - zi[..., None] * bi
    bbi = zr[..., None] * bi + zi[..., None] * br
    eye = jnp.eye(S5_GROUPS, dtype=F32)
    def bd_in(m):
        return jnp.einsum('gnh,gk->ghkn', m, eye).reshape(GROUP, S5_GROUPS * S5_N)
    def bd_out(m):
        return jnp.einsum('ghn,gk->gnkh', m, eye).reshape(S5_GROUPS * S5_N, GROUP)
    bbar = jnp.concatenate([bd_in(bbr), bd_in(bbi)], axis=1).astype(BF16)
    cmat = jnp.concatenate([bd_out(c_re.astype(F32)), -bd_out(c_im.astype(F32))],
                           axis=0).astype(BF16)
    return bbar, ar.reshape(1, -1), ai.reshape(1, -1), cmat


def _outproj_kernel(yf_ref, yh_ref, yd_ref, ys_ref, wo_ref, x_ref, pg_ref, g1_ref, g2_ref,
                    sc_ref, sh_ref, rw_ref, rb_ref,
                    x1_ref, h2_ref, idx_ref, wt_ref, cnt_ref, carry_ref):
    tm = x_ref.shape[0]

    @pl.when((pl.program_id(0) == 0) & (pl.program_id(1) == 0))
    def _():
        carry_ref[...] = jnp.zeros_like(carry_ref)

    y = (_dot(yf_ref[...], wo_ref[0:GROUP, :])
         + _dot(yh_ref[...], wo_ref[GROUP:2 * GROUP, :])
         + _dot(yd_ref[...], wo_ref[2 * GROUP:3 * GROUP, :])
         + _dot(ys_ref[...], wo_ref[3 * GROUP:4 * GROUP, :]))
    ms = jnp.mean(y * y, axis=-1, keepdims=True)
    x1 = x_ref[...] + g1_ref[0] * (y * lax.rsqrt(ms + EPS) * pg_ref[...])
    x1_ref[...] = x1
    ms2 = jnp.mean(x1 * x1, axis=-1, keepdims=True)
    h2 = (x1 * lax.rsqrt(ms2 + EPS) * g2_ref[...]) * (1.0 + sc_ref[0]) + sh_ref[0]
    for c in range(ROW_SPLIT):
        h2_ref[pl.ds(c, tm, stride=ROW_SPLIT), :] = h2[:, c * LANES:(c + 1) * LANES]
    h_hi = h2.astype(BF16)
    h_lo = (h2 - h_hi.astype(F32)).astype(BF16)
    rw = rw_ref[...]
    w_hi = rw.astype(BF16)
    w_lo = (rw - w_hi.astype(F32)).astype(BF16)
    logits = _dot(h_hi, w_hi) + (_dot(h_lo, w_hi) + _dot(h_hi, w_lo)) + rb_ref[...]
    lane = lax.broadcasted_iota(jnp.int32, (tm, LANES), 1)
    cur = logits
    vals, idxs = [], []
    for _ in range(TOP_K):
        m = jnp.max(cur, axis=-1, keepdims=True)
        sel = jnp.min(jnp.where(cur == m, lane, LANES), axis=-1, keepdims=True)
        vals.append(m)
        idxs.append(sel)
        cur = jnp.where(lane == sel, -jnp.inf, cur)
    es = [jnp.exp(v - vals[0]) for v in vals]
    tot = es[0] + es[1] + es[2] + es[3]
    hits = (lane == idxs[0]) | (lane == idxs[1]) | (lane == idxs[2]) | (lane == idxs[3])
    cnt = jnp.where(hits, 1.0, 0.0)
    r = lax.broadcasted_iota(jnp.int32, (tm, tm), 0)
    cidx = lax.broadcasted_iota(jnp.int32, (tm, tm), 1)
    below = jnp.where(cidx < r, 1.0, 0.0).astype(BF16)
    before = _dot(below, cnt.astype(BF16)) + carry_ref[...]
    total = carry_ref[...] + jnp.sum(cnt, axis=0, keepdims=True)
    carry_ref[...] = total
    cnt_ref[...] = total
    idx_out = jnp.zeros((tm, LANES), jnp.int32)
    wt_out = jnp.zeros((tm, LANES), F32)
    for k in range(TOP_K):
        rank = jnp.sum(jnp.where(lane == idxs[k], before, 0.0), axis=-1, keepdims=True)
        idx_out = jnp.where(lane == k, idxs[k], idx_out)
        idx_out = jnp.where(lane == TOP_K + k, rank.astype(jnp.int32), idx_out)
        wt_out = jnp.where(lane == k, es[k] / tot, wt_out)
    idx_ref[...] = idx_out
    wt_ref[...] = wt_out


def _outproj_call(yf, yh, yd, ys, w_out, x2, post_g, g1, pre_g2, sc2, sh2, rw, rb, batch, seq):
    t = x2.shape[0]
    tm = min(ROW_TILE, seq)
    nt = seq // tm
    row = lambda b, i: (b * nt + i, 0)
    const2 = lambda b, i: (0, 0)
    per_b = lambda b, i: (b, 0, 0)
    grp = pl.BlockSpec((tm, GROUP), row)
    vec = pl.BlockSpec((1, D_MODEL), const2)
    mod = pl.BlockSpec((1, 1, D_MODEL), per_b)
    return pl.pallas_call(
        _outproj_kernel,
        grid=(batch, nt),
        in_specs=[grp, grp, grp, grp,
                  pl.BlockSpec((D_MODEL, D_MODEL), const2),
                  pl.BlockSpec((tm, D_MODEL), row),
                  vec, mod, vec, mod, mod,
                  pl.BlockSpec((D_MODEL, LANES), const2),
                  pl.BlockSpec((1, LANES), const2)],
        out_specs=[pl.BlockSpec((tm, D_MODEL), row),
                   pl.BlockSpec((tm * ROW_SPLIT, LANES), row),
                   pl.BlockSpec((tm, LANES), row), pl.BlockSpec((tm, LANES), row),
                   pl.BlockSpec((1, LANES), const2)],
        out_shape=[jax.ShapeDtypeStruct((t, D_MODEL), F32),
                   jax.ShapeDtypeStruct((t * ROW_SPLIT, LANES), F32),
                   jax.ShapeDtypeStruct((t, LANES), jnp.int32),
                   jax.ShapeDtypeStruct((t, LANES), F32),
                   jax.ShapeDtypeStruct((1, LANES), F32)],
        scratch_shapes=[pltpu.VMEM((1, LANES), F32)],
        compiler_params=_cparams(("arbitrary", "arbitrary")),
    )(yf, yh, yd, ys, w_out, x2, post_g, g1, pre_g2, sc2, sh2, rw, rb)


def _row_copy(src_hbm, dst_vmem, src_row, dst_row, sem):
    s0 = pl.multiple_of(src_row * ROW_SPLIT, ROW_SPLIT)
    return pltpu.make_async_copy(src_hbm.at[pl.ds(s0, ROW_SPLIT), :],
                                 dst_vmem.at[pl.ds(dst_row * ROW_SPLIT, ROW_SPLIT), :], sem)


def _start_rows(idx_ref, n, src_hbm, dst_vmem, sem, idx_off=0):
    for j in range(n):
        _row_copy(src_hbm, dst_vmem, idx_ref[0, 0, idx_off + j], j, sem).start()


def _wait_rows(n, src_hbm, dst_vmem, sem):
    pltpu.make_async_copy(src_hbm.at[pl.ds(0, n * ROW_SPLIT), :], dst_vmem, sem).wait()


def _tile_rows(buf, n):
    return jnp.concatenate([buf[pl.ds(c, n, stride=ROW_SPLIT), :] for c in range(ROW_SPLIT)],
                           axis=1)


def _expert_kernel(te_ref, src_ref, nxt_ref, h_hbm, w1_ref, p_ref, b1_ref, w2_ref, b2_ref,
                   o_ref, xa, xb, w1_s, w2_s, sem):
    i = pl.program_id(0)
    last = pl.num_programs(0) - 1
    tm = EXP_TILE

    @pl.when(i == 0)
    def _():
        _start_rows(src_ref, tm, h_hbm, xa, sem.at[0])

    @pl.when((i == 0) | (te_ref[i] != te_ref[jnp.maximum(i - 1, 0)]))
    def _():
        chunk = 2 * LANES
        for c in range(2 * D_FF // chunk):
            r = _dot(w1_ref[0, 0, :, c * chunk:(c + 1) * chunk].astype(BF16), p_ref[...])
            w1_s[:, c * LANES:(c + 1) * LANES] = r[:, 0:LANES].astype(BF16)
            w1_s[:, D_FF + c * LANES:D_FF + (c + 1) * LANES] = r[:, LANES:chunk].astype(BF16)
        w2_s[...] = w2_ref[0, 0].astype(BF16)

    def step(cur, cur_sem, nxt, nxt_sem):
        _wait_rows(tm, h_hbm, cur, cur_sem)
        _start_rows(nxt_ref, tm, h_hbm, nxt, nxt_sem)
        hh = _dot(_tile_rows(cur, tm).astype(BF16), w1_s[...]) + b1_ref[0]
        glu = jnp.minimum(hh[:, 0:D_FF], SWIGLU_LIMIT)
        lin = jnp.clip(hh[:, D_FF:2 * D_FF], -SWIGLU_LIMIT, SWIGLU_LIMIT)
        act = glu * _sigmoid(SWIGLU_ALPHA * glu) * (lin + 1.0)
        y = _dot(act.astype(BF16), w2_s[...]) + b2_ref[0]
        for c in range(ROW_SPLIT):
            o_ref[pl.ds(c, tm, stride=ROW_SPLIT), :] = y[:, c * LANES:(c + 1) * LANES]

        @pl.when(i == last)
        def _():
            _wait_rows(tm, h_hbm, nxt, nxt_sem)

    @pl.when(lax.rem(i, 2) == 0)
    def _():
        step(xa, sem.at[0], xb, sem.at[1])

    @pl.when(lax.rem(i, 2) == 1)
    def _():
        step(xb, sem.at[1], xa, sem.at[0])


def _expert_call(tile_expert, src_rows, h2, w1, b1, w2, b2, layer):
    ntiles = tile_expert.shape[0]
    tm = EXP_TILE
    src3 = src_rows.reshape(ntiles, 1, tm)
    perm = np.zeros((2 * LANES, 2 * LANES), np.float32)
    perm[2 * np.arange(LANES), np.arange(LANES)] = 1.0
    perm[2 * np.arange(LANES) + 1, LANES + np.arange(LANES)] = 1.0
    grid_spec = pltpu.PrefetchScalarGridSpec(
        num_scalar_prefetch=1,
        grid=(ntiles,),
        in_specs=[pl.BlockSpec((1, 1, tm), lambda i, te: (i, 0, 0), memory_space=pltpu.SMEM),
                  pl.BlockSpec((1, 1, tm), lambda i, te: (jnp.minimum(i + 1, ntiles - 1), 0, 0),
                               memory_space=pltpu.SMEM),
                  pl.BlockSpec(memory_space=pl.ANY),
                  pl.BlockSpec((1, 1, D_MODEL, 2 * D_FF), lambda i, te: (layer, te[i], 0, 0)),
                  pl.BlockSpec((2 * LANES, 2 * LANES), lambda i, te: (0, 0)),
                  pl.BlockSpec((1, 1, 2 * D_FF), lambda i, te: (te[i], 0, 0)),
                  pl.BlockSpec((1, 1, D_FF, D_MODEL), lambda i, te: (layer, te[i], 0, 0)),
                  pl.BlockSpec((1, 1, D_MODEL), lambda i, te: (te[i], 0, 0))],
        out_specs=pl.BlockSpec((tm * ROW_SPLIT, LANES), lambda i, te: (i, 0)),
        scratch_shapes=[pltpu.VMEM((tm * ROW_SPLIT, LANES), F32),
                        pltpu.VMEM((tm * ROW_SPLIT, LANES), F32),
                        pltpu.VMEM((D_MODEL, 2 * D_FF), BF16),
                        pltpu.VMEM((D_FF, D_MODEL), BF16),
                        pltpu.SemaphoreType.DMA((2,))],
    )
    return pl.pallas_call(
        _expert_kernel,
        grid_spec=grid_spec,
        out_shape=jax.ShapeDtypeStruct((ntiles * tm * ROW_SPLIT, LANES), F32),
        compiler_params=_cparams(("arbitrary",)),
    )(tile_expert, src3, src3, h2, w1, jnp.asarray(perm, BF16), b1, w2, b2)


CMB_CHUNK = 16
CMB_NCHUNK = CMB_TILE * TOP_K // CMB_CHUNK + N_EXPERTS


def _chunk_copy(y_hbm, buf, src_row, slot, sem):
    rows = CMB_CHUNK * ROW_SPLIT
    s0 = pl.multiple_of(src_row * ROW_SPLIT, ROW_SPLIT)
    return pltpu.make_async_copy(y_hbm.at[pl.ds(s0, rows), :],
                                 buf.at[pl.ds(slot * rows, rows), :], sem)


def _combine_kernel(ch_ref, nxt_ref, loc_ref, y_hbm, wt_ref, x_ref, pg_ref, g_ref, o_ref,
                    sa, sb, ybuf, sem):
    i = pl.program_id(0) * pl.num_programs(1) + pl.program_id(1)
    last = pl.num_programs(0) * pl.num_programs(1) - 1
    tm = x_ref.shape[0]

    def start(idx_ref, buf, s):
        for c in range(CMB_NCHUNK):
            _chunk_copy(y_hbm, buf, idx_ref[0, 0, c], c, s).start()

    def wait(buf, s):
        pltpu.make_async_copy(y_hbm.at[pl.ds(0, buf.shape[0]), :], buf, s).wait()

    @pl.when(i == 0)
    def _():
        start(ch_ref, sa, sem.at[0])

    def step(cur, cur_sem, nxt, nxt_sem):
        wait(cur, cur_sem)
        start(nxt_ref, nxt, nxt_sem)
        for k in range(TOP_K):
            for j in range(tm):
                r0 = pl.multiple_of(loc_ref[0, 0, k * tm + j] * ROW_SPLIT, ROW_SPLIT)
                ybuf[k, j * ROW_SPLIT:(j + 1) * ROW_SPLIT, :] = cur[pl.ds(r0, ROW_SPLIT), :]
        wt = wt_ref[...]
        y = wt[:, 0:1] * _tile_rows(ybuf.at[0], tm)
        for k in range(1, TOP_K):
            y = y + wt[:, k:k + 1] * _tile_rows(ybuf.at[k], tm)
        ms = jnp.mean(y * y, axis=-1, keepdims=True)
        o_ref[...] = x_ref[...] + g_ref[0] * (y * lax.rsqrt(ms + EPS) * pg_ref[...])

        @pl.when(i == last)
        def _():
            wait(nxt, nxt_sem)

    @pl.when(lax.rem(i, 2) == 0)
    def _():
        step(sa, sem.at[0], sb, sem.at[1])

    @pl.when(lax.rem(i, 2) == 1)
    def _():
        step(sb, sem.at[1], sa, sem.at[0])


def _combine_plan(idx, dest, tile_start, t):
    tm = CMB_TILE
    nblk = t // tm
    idx_b = idx.reshape(nblk, tm * TOP_K)
    cnt = jnp.sum(jax.nn.one_hot(idx_b, N_EXPERTS, dtype=jnp.int32), axis=1)
    before = jnp.cumsum(cnt, axis=0) - cnt
    run_start = tile_start[None, :] * EXP_TILE + before
    nch = (cnt + CMB_CHUNK - 1) // CMB_CHUNK
    ch_end = jnp.cumsum(nch, axis=1)
    ch_off = ch_end - nch
    slot = jnp.arange(CMB_NCHUNK, dtype=jnp.int32)
    e_of = jnp.minimum(jnp.sum((slot[None, None, :] >= ch_end[:, :, None]).astype(jnp.int32), axis=1),
                       N_EXPERTS - 1)
    c_in = slot[None, :] - jnp.take_along_axis(ch_off, e_of, axis=1)
    src = jnp.take_along_axis(run_start, e_of, axis=1) + c_in * CMB_CHUNK
    src = jnp.where(slot[None, :] < ch_end[:, -1:], src, 0).astype(jnp.int32)
    loc = (dest.reshape(nblk, tm * TOP_K) - jnp.take_along_axis(run_start, idx_b, axis=1)
           + jnp.take_along_axis(ch_off, idx_b, axis=1) * CMB_CHUNK).astype(jnp.int32)
    loc = loc.reshape(nblk, tm, TOP_K).transpose(0, 2, 1).reshape(nblk, 1, TOP_K * tm)
    return src.reshape(nblk, 1, CMB_NCHUNK), loc


def _combine_call(chunk_src, loc, y_sorted, wts, x1, post_g, g2, batch, seq):
    t = x1.shape[0]
    tm = CMB_TILE
    nt = seq // tm
    nblk = t // tm
    stage_rows = CMB_NCHUNK * CMB_CHUNK * ROW_SPLIT
    row = lambda b, i: (b * nt + i, 0)
    return pl.pallas_call(
        _combine_kernel,
        grid=(batch, nt),
        in_specs=[pl.BlockSpec((1, 1, CMB_NCHUNK), lambda b, i: (b * nt + i, 0, 0),
                               memory_space=pltpu.SMEM),
                  pl.BlockSpec((1, 1, CMB_NCHUNK),
                               lambda b, i: (jnp.minimum(b * nt + i + 1, nblk - 1), 0, 0),
                               memory_space=pltpu.SMEM),
                  pl.BlockSpec((1, 1, TOP_K * tm), lambda b, i: (b * nt + i, 0, 0),
                               memory_space=pltpu.SMEM),
                  pl.BlockSpec(memory_space=pl.ANY),
                  pl.BlockSpec((tm, LANES), row),
                  pl.BlockSpec((tm, D_MODEL), row),
                  pl.BlockSpec((1, D_MODEL), lambda b, i: (0, 0)),
                  pl.BlockSpec((1, 1, D_MODEL), lambda b, i: (b, 0, 0))],
        out_specs=pl.BlockSpec((tm, D_MODEL), row),
        out_shape=jax.ShapeDtypeStruct((t, D_MODEL), F32),
        scratch_shapes=[pltpu.VMEM((stage_rows, LANES), F32),
                        pltpu.VMEM((stage_rows, LANES), F32),
                        pltpu.VMEM((TOP_K, tm * ROW_SPLIT, LANES), F32),
                        pltpu.SemaphoreType.DMA((2,))],
        compiler_params=_cparams(("arbitrary", "arbitrary")),
    )(chunk_src, chunk_src, loc, y_sorted, wts, x1, post_g, g2)


def _dispatch_plan(idx, rank, counts, t):
    tm = EXP_TILE
    npairs = t * TOP_K
    ntiles = npairs // tm + N_EXPERTS + 1
    e_flat = idx.reshape(-1)
    pair = jnp.arange(npairs, dtype=jnp.int32)
    _, order = lax.sort((e_flat, pair), num_keys=1, is_stable=True)
    cnt_excl = jnp.cumsum(counts) - counts
    tiles_per = (counts + tm - 1) // tm
    tile_end = jnp.cumsum(tiles_per)
    tile_start = tile_end - tiles_per
    tile_ids = jnp.arange(ntiles, dtype=jnp.int32)
    tile_expert = jnp.minimum(
        jnp.sum((tile_ids[:, None] >= tile_end[None, :]).astype(jnp.int32), axis=1),
        N_EXPERTS - 1).astype(jnp.int32)
    rows = jnp.arange(ntiles * tm, dtype=jnp.int32)
    row_e = jnp.repeat(tile_expert, tm)
    off = rows - tile_start[row_e] * tm
    valid = (off < counts[row_e]) & (rows < tile_end[-1] * tm)
    sidx = jnp.clip(cnt_excl[row_e] + off, 0, npairs - 1)
    src_rows = jnp.where(valid, order[sidx] // TOP_K, 0).astype(jnp.int32)
    dest = (tile_start[idx] * tm + rank).astype(jnp.int32)
    return tile_expert, src_rows, dest, tile_start


def _rope_tables():
    lane = np.arange(GROUP)
    d = lane % DIFF_D
    rd = DIFF_D // 4
    half = rd // 2
    inv = np.where(d < rd, ROPE_THETA ** (-(d % half).astype(np.float32) / half), 0.0)
    ma = np.where(d < half, -1.0, 0.0)
    mb = np.where((d >= half) & (d < rd), 1.0, 0.0)
    f = lambda a: jnp.asarray(a.reshape(1, GROUP), F32)
    return f(inv), f(ma), f(mb)


def kernel(x, c, positions, ada_w, ada_b, pre_norm_g, post_norm_g, w_in, w_out, fox_fb, hg_lower, hg_norm_g, diff_lam_q1, diff_lam_k1, diff_lam_q2, diff_lam_k2, diff_subln_g, s5_a_re, s5_a_im, s5_log_step, s5_b_re, s5_b_im, s5_c_re, s5_c_im, s5_d, s5_glu_w, router_w, router_b, exp_w1, exp_b1, exp_w2, exp_b2):
    batch, seq, d = x.shape
    depth = ada_w.shape[0]
    t = batch * seq
    assert d == D_MODEL and seq % ROW_TILE == 0

    lb_all = jnp.cumsum(jax.nn.softmax(hg_lower.astype(F32), axis=0), axis=0)
    lb_all = lb_all - lb_all[0:1]
    mod = _ada_call(c, ada_w, ada_b)
    inv_lane, mask_a, mask_b = _rope_tables()
    pos = positions.reshape(t, 1).astype(jnp.int32)
    xf = x.reshape(t, d)

    w_main, w_f = _win_prep_call(w_in)
    for l in range(depth):
        m6 = mod[l].reshape(batch, 6, 1, d)
        sh1, sc1, g1, sh2, sc2, g2 = [m6[:, j] for j in range(6)]
        fb = jnp.pad(fox_fb[l], (0, LANES - FOX_HEADS)).reshape(1, LANES)
        fox, frow, hg, diff, s5u = _inproj_call(
            xf, pre_norm_g[l, 0].reshape(1, d), sc1, sh1, w_main, w_f, fb, pos,
            inv_lane, mask_a, mask_b, batch, seq, l)

        y_fox = _fox_call(fox, frow, batch, seq)

        lam_init = 0.8 - 0.6 * math.exp(-0.3 * l)
        lam = (jnp.exp(jnp.sum(diff_lam_q1[l].astype(F32) * diff_lam_k1[l].astype(F32)))
               - jnp.exp(jnp.sum(diff_lam_q2[l].astype(F32) * diff_lam_k2[l].astype(F32))) + lam_init)
        y_diff = _diff_call(diff, lam.reshape(1), diff_subln_g[l].reshape(1, DIFF_DV),
                            1.0 - lam_init, batch, seq)

        y_hg = _hgrn2_call(hg, lb_all[l].reshape(1, -1), hg_norm_g[l].reshape(1, HG_DV), batch, seq)

        bbar, ar, ai, cmat = _s5_params(s5_a_re[l], s5_a_im[l], s5_log_step[l], s5_b_re[l],
                                        s5_b_im[l], s5_c_re[l], s5_c_im[l])
        u_sb = s5u.reshape(batch, seq, GROUP).transpose(1, 0, 2).reshape(t, GROUP)
        y_s5_sb = _s5_call(u_sb, bbar, ar, ai, cmat, s5_d[l].reshape(1, GROUP),
                           s5_glu_w[l].astype(BF16), batch, seq)
        y_s5 = y_s5_sb.reshape(seq, batch, GROUP).transpose(1, 0, 2).reshape(t, GROUP)

        rw = jnp.pad(router_w[l], ((0, 0), (0, LANES - N_EXPERTS)))
        rb = jnp.pad(router_b[l], (0, LANES - N_EXPERTS), constant_values=NEG_BIG).reshape(1, LANES)
        x1, h2, idx, wts, cnt = _outproj_call(
            y_fox, y_hg, y_diff, y_s5, w_out[l].astype(BF16), xf,
            post_norm_g[l, 0].reshape(1, d), g1, pre_norm_g[l, 1].reshape(1, d), sc2, sh2,
            rw, rb, batch, seq)

        tile_expert, src_rows, dest, tile_start = _dispatch_plan(
            idx[:, 0:TOP_K], idx[:, TOP_K:2 * TOP_K], cnt[0, 0:N_EXPERTS].astype(jnp.int32), t)
        chunk_src, loc = _combine_plan(idx[:, 0:TOP_K], dest, tile_start, t)
        b1 = exp_b1[l]
        b1p = jnp.concatenate([b1[:, 0::2], b1[:, 1::2]], axis=-1).reshape(N_EXPERTS, 1, 2 * D_FF)
        y_sorted = _expert_call(tile_expert, src_rows, h2, exp_w1, b1p, exp_w2,
                                exp_b2[l].reshape(N_EXPERTS, 1, D_MODEL), l)
        xf = _combine_call(chunk_src, loc, y_sorted, wts, x1, post_norm_g[l, 1].reshape(1, d), g2,
                           batch, seq)

    return xf.reshape(batch, seq, d)
```

```python
import functools
import math

import numpy as np
import jax
import jax.numpy as jnp
from jax import lax
from jax.experimental import pallas as pl
from jax.experimental.pallas import tpu as pltpu

F32 = jnp.float32
BF16 = jnp.bfloat16

D_MODEL = 1024
GROUP = 256
FOX_HEADS = 4
FOX_HD = 64
HG_HEADS = 4
HG_DK = 128
HG_DV = 64
DIFF_HEADS = 4
DIFF_DV = 64
DIFF_D = 32
S5_CH = 16
S5_GROUPS = 16
S5_N = 64
ROPE_THETA = 500000.0
N_EXPERTS = 32
TOP_K = 4
D_FF = 1024
SWIGLU_LIMIT = 7.0
SWIGLU_ALPHA = 1.702
EPS = 1e-6
NEG_BIG = -1e30

LANES = 128
VMEM_LIMIT = 56 * 1024 * 1024

LOG2E = 1.4426950408889634

ROW_TILE = 512
ATT_T = ROW_TILE
HG_CHUNK = 64
S5_STEPS = 128
EXP_TILE = 256
CMB_TILE = 256

ROW_SPLIT = D_MODEL // LANES

HG_COLS = 2 * HG_HEADS * HG_DK + 2 * GROUP
MAIN_COLS = 3 * GROUP + HG_COLS + 3 * GROUP + GROUP


def _cparams(sem):
    return pltpu.CompilerParams(dimension_semantics=sem, vmem_limit_bytes=VMEM_LIMIT)


def _sigmoid(x):
    return 1.0 / (1.0 + jnp.exp(-x))


def _shr(x, pow2):
    return lax.shift_right_logical(x, int(math.log2(pow2)))


def _dot(a, b):
    return jnp.dot(a, b, preferred_element_type=F32)


def _dot_nt(a, b):
    return lax.dot_general(a, b, (((1,), (1,)), ((), ())), preferred_element_type=F32)


def _dot_tn(a, b):
    return lax.dot_general(a, b, (((0,), (0,)), ((), ())), preferred_element_type=F32)


def _split_dot(m_bf16, x, terms=3):
    out = None
    for _ in range(terms):
        piece = x.astype(BF16)
        x = x - piece.astype(F32)
        d = _dot(m_bf16, piece)
        out = d if out is None else out + d
    return out


def _ada_kernel(c_ref, w_ref, b_ref, o_ref):
    c = c_ref[...]
    cs = c * _sigmoid(c)
    o_ref[0] = jnp.dot(cs, w_ref[0], preferred_element_type=F32,
                       precision=lax.Precision.HIGHEST) + b_ref[0]


def _ada_call(c, ada_w, ada_b):
    depth, d, n = ada_w.shape
    b = c.shape[0]
    tn = 1536
    return pl.pallas_call(
        _ada_kernel,
        grid=(depth, n // tn),
        in_specs=[pl.BlockSpec((b, d), lambda l, j: (0, 0)),
                  pl.BlockSpec((1, d, tn), lambda l, j: (l, 0, j)),
                  pl.BlockSpec((1, 1, tn), lambda l, j: (l, 0, j))],
        out_specs=pl.BlockSpec((1, b, tn), lambda l, j: (l, 0, j)),
        out_shape=jax.ShapeDtypeStruct((depth, b, n), F32),
        compiler_params=_cparams(("arbitrary", "arbitrary")),
    )(c, ada_w, ada_b.reshape(depth, 1, n))


_FOX_COLS = 3 * GROUP


def _win_prep_kernel(w_ref, main_ref, f_ref):
    rows = w_ref.shape[1]
    lane = lax.broadcasted_iota(jnp.int32, (rows, LANES), 1)
    main_ref[0, :, 0:_FOX_COLS] = w_ref[0, :, 0:_FOX_COLS].astype(BF16)
    first = w_ref[0, :, _FOX_COLS:_FOX_COLS + LANES]
    f_ref[0] = jnp.where(lane < FOX_HEADS, first, 0.0).astype(BF16)
    cur = pltpu.roll(first, LANES - FOX_HEADS, axis=1)
    for m in range((MAIN_COLS - _FOX_COLS) // LANES):
        c0 = _FOX_COLS + m * LANES
        nxt = pltpu.roll(w_ref[0, :, c0 + LANES:c0 + 2 * LANES], LANES - FOX_HEADS, axis=1)
        main_ref[0, :, c0:c0 + LANES] = jnp.where(lane < LANES - FOX_HEADS, cur, nxt).astype(BF16)
        cur = nxt


def _win_prep_call(w_in):
    depth, d, cols = w_in.shape
    padded = MAIN_COLS + LANES
    w_pad = jnp.pad(w_in, ((0, 0), (0, 0), (0, padded - cols)))
    rows = 256
    return pl.pallas_call(
        _win_prep_kernel,
        grid=(depth, d // rows),
        in_specs=[pl.BlockSpec((1, rows, padded), lambda l, r: (l, r, 0))],
        out_specs=[pl.BlockSpec((1, rows, MAIN_COLS), lambda l, r: (l, r, 0)),
                   pl.BlockSpec((1, rows, LANES), lambda l, r: (l, r, 0))],
        out_shape=[jax.ShapeDtypeStruct((depth, d, MAIN_COLS), BF16),
                   jax.ShapeDtypeStruct((depth, d, LANES), BF16)],
        compiler_params=_cparams(("arbitrary", "arbitrary")),
    )(w_pad)


def _inproj_kernel(x_ref, g_ref, sc_ref, sh_ref, w_ref, wf_ref, fb_ref, pos_ref, inv_ref,
                   ma_ref, mb_ref,
                   fox_ref, frow_ref, hg_ref, diff_ref, s5_ref, carry_ref):
    i = pl.program_id(1)
    tm = x_ref.shape[0]

    @pl.when(i == 0)
    def _():
        carry_ref[...] = jnp.zeros_like(carry_ref)

    x = x_ref[...]
    ms = jnp.mean(x * x, axis=-1, keepdims=True)
    h = (x * lax.rsqrt(ms + EPS) * g_ref[...]) * (1.0 + sc_ref[0]) + sh_ref[0]
    hb = h.astype(BF16)

    o = 0
    pf = _dot(hb, w_ref[0, :, o:o + 3 * GROUP])
    fox_ref[:, 0:GROUP] = (pf[:, 0:GROUP] * (FOX_HD ** -0.5 * LOG2E)).astype(BF16)
    fox_ref[:, GROUP:3 * GROUP] = pf[:, GROUP:3 * GROUP].astype(BF16)
    o += 3 * GROUP

    ff = _dot(hb, wf_ref[0]) + fb_ref[...]
    logf = jnp.minimum(ff, 0.0) - jnp.log(1.0 + jnp.exp(-jnp.abs(ff)))
    r = lax.broadcasted_iota(jnp.int32, (tm, tm), 0)
    cidx = lax.broadcasted_iota(jnp.int32, (tm, tm), 1)
    tri = jnp.where(cidx <= r, 1.0, 0.0).astype(BF16)
    cum = _split_dot(tri, logf) + carry_ref[...]
    carry_ref[...] = cum[tm - 1:tm, :]
    cum_t = jnp.transpose(cum * LOG2E)
    frow_ref[0, 0] = cum_t[0:8, :]

    hg_ref[...] = _dot(hb, w_ref[0, :, o:o + HG_COLS]).astype(BF16)
    o += HG_COLS

    pd = _dot(hb, w_ref[0, :, o:o + 3 * GROUP])
    o += 3 * GROUP
    pos = pos_ref[...].astype(F32)
    ang = pos * inv_ref[...]
    cos = jnp.cos(ang)
    sin = jnp.sin(ang)
    ma = ma_ref[...]
    mb = mb_ref[...]

    def rope(xx):
        outs = []
        for j in range(GROUP // LANES):
            sl = slice(j * LANES, (j + 1) * LANES)
            xs = xx[:, sl]
            partner = (pltpu.roll(xs, LANES - 4, axis=1) * ma[:, sl]
                       + pltpu.roll(xs, 4, axis=1) * mb[:, sl])
            outs.append(xs * cos[:, sl] + partner * sin[:, sl])
        return jnp.concatenate(outs, axis=1)

    diff_ref[:, 0:GROUP] = (rope(pd[:, 0:GROUP]) * (DIFF_D ** -0.5 * LOG2E)).astype(BF16)
    diff_ref[:, GROUP:2 * GROUP] = rope(pd[:, GROUP:2 * GROUP]).astype(BF16)
    diff_ref[:, 2 * GROUP:3 * GROUP] = pd[:, 2 * GROUP:3 * GROUP].astype(BF16)

    s5_ref[...] = _dot(hb, w_ref[0, :, o:o + GROUP]).astype(BF16)


def _inproj_call(x2, g_pre, sc, sh, w_main, w_f, fb, pos, inv_lane, mask_a, mask_b, batch, seq,
                 layer):
    t = x2.shape[0]
    tm = ROW_TILE
    nt = seq // tm
    row = lambda b, i: (b * nt + i, 0)
    const2 = lambda b, i: (0, 0)
    per_b = lambda b, i: (b, 0, 0)
    outs = pl.pallas_call(
        _inproj_kernel,
        grid=(batch, nt),
        in_specs=[pl.BlockSpec((tm, D_MODEL), row),
                  pl.BlockSpec((1, D_MODEL), const2),
                  pl.BlockSpec((1, 1, D_MODEL), per_b),
                  pl.BlockSpec((1, 1, D_MODEL), per_b),
                  pl.BlockSpec((1, D_MODEL, MAIN_COLS), lambda b, i: (layer, 0, 0)),
                  pl.BlockSpec((1, D_MODEL, LANES), lambda b, i: (layer, 0, 0)),
                  pl.BlockSpec((1, LANES), const2),
                  pl.BlockSpec((tm, 1), row),
                  pl.BlockSpec((1, GROUP), const2),
                  pl.BlockSpec((1, GROUP), const2),
                  pl.BlockSpec((1, GROUP), const2)],
        out_specs=[pl.BlockSpec((tm, 3 * GROUP), row),
                   pl.BlockSpec((1, 1, 8, tm), lambda b, i: (b, i, 0, 0)),
                   pl.BlockSpec((tm, HG_COLS), row),
                   pl.BlockSpec((tm, 3 * GROUP), row),
                   pl.BlockSpec((tm, GROUP), row)],
        out_shape=[jax.ShapeDtypeStruct((t, 3 * GROUP), BF16),
                   jax.ShapeDtypeStruct((batch, nt, 8, tm), F32),
                   jax.ShapeDtypeStruct((t, HG_COLS), BF16),
                   jax.ShapeDtypeStruct((t, 3 * GROUP), BF16),
                   jax.ShapeDtypeStruct((t, GROUP), BF16)],
        scratch_shapes=[pltpu.VMEM((1, LANES), F32)],
        compiler_params=_cparams(("arbitrary", "arbitrary")),
    )(x2, g_pre, sc, sh, w_main, w_f, fb, pos, inv_lane, mask_a, mask_b)
    return outs


_HEAD_W = 64
_HEADS = 4


def _slot(x64):
    return jnp.concatenate([x64, jnp.zeros_like(x64)], axis=1)


def _pad_kv(k_ref, v_ref, k_s, v_s):
    seq = k_ref.shape[0]
    rows = ATT_T
    lane = lax.broadcasted_iota(jnp.int32, (rows, LANES), 1)

    def body(c, _):
        r0 = pl.multiple_of(c * rows, rows)
        for h in range(_HEADS):
            k_s[pl.ds(r0, rows), h * LANES:(h + 1) * LANES] = _slot(
                k_ref[pl.ds(r0, rows), h * _HEAD_W:(h + 1) * _HEAD_W])
            v = _slot(v_ref[pl.ds(r0, rows), h * _HEAD_W:(h + 1) * _HEAD_W])
            v_s[pl.ds(r0, rows), h * LANES:(h + 1) * LANES] = jnp.where(
                lane == _HEAD_W, jnp.ones_like(v), v)
        return 0

    lax.fori_loop(0, seq // rows, body, 0)


def _attend(qs, slots, scratch, qi, key_bias):
    k_s, v_s = scratch
    n = len(qs)
    tq = qs[0].shape[0]
    tk = ATT_T

    def block(kb, carry, keep):
        ms, accs = carry
        start = pl.multiple_of(kb * tk, tk)
        new_m, new_acc = [], []
        for j in range(n):
            c0 = slots[j] * LANES
            s = _dot_nt(qs[j], k_s[pl.ds(start, tk), c0:c0 + LANES])
            kbias = key_bias(j, kb)
            if kbias is not None:
                s = s - kbias
            if keep is not None:
                s = jnp.where(keep, s, NEG_BIG)
            m_new = jnp.maximum(ms[j], jnp.max(s, axis=-1, keepdims=True))
            alpha = jnp.exp2(ms[j] - m_new)
            p = jnp.exp2(s - m_new).astype(BF16)
            new_acc.append(alpha * accs[j] + _dot(p, v_s[pl.ds(start, tk), c0:c0 + LANES]))
            new_m.append(m_new)
        return tuple(new_m), tuple(new_acc)

    init = (tuple(jnp.full((tq, 1), NEG_BIG, F32) for _ in range(n)),
            tuple(jnp.zeros((tq, LANES), F32) for _ in range(n)))
    carry = lax.fori_loop(0, qi, lambda kb, c: block(kb, c, None), init)
    causal = (lax.broadcasted_iota(jnp.int32, (tq, tk), 1)
              <= lax.broadcasted_iota(jnp.int32, (tq, tk), 0))
    return block(qi, carry, causal)[1]


def _fox_kernel(q_ref, k_ref, v_ref, frow_ref, o_ref, k_s, v_s):
    qi = pl.program_id(1)

    @pl.when(qi == 0)
    def _():
        _pad_kv(k_ref, v_ref, k_s, v_s)

    qs = [_slot(q_ref[:, h * _HEAD_W:(h + 1) * _HEAD_W]) for h in range(_HEADS)]
    accs = _attend(qs, list(range(_HEADS)), (k_s, v_s), qi,
                   lambda j, kb: frow_ref[0, kb, j:j + 1, :])
    outs = [a[:, 0:_HEAD_W] / a[:, _HEAD_W:_HEAD_W + 1] for a in accs]
    o_ref[...] = jnp.concatenate(outs, axis=1).astype(BF16)


def _att_specs(seq):
    nq = seq // ATT_T
    return ([pl.BlockSpec((ATT_T, GROUP), lambda b, i: (b * nq + i, 0)),
             pl.BlockSpec((seq, GROUP), lambda b, i: (b, 1)),
             pl.BlockSpec((seq, GROUP), lambda b, i: (b, 2))],
            pl.BlockSpec((ATT_T, GROUP), lambda b, i: (b * nq + i, 0)),
            [pltpu.VMEM((seq, _HEADS * LANES), BF16), pltpu.VMEM((seq, _HEADS * LANES), BF16)])


def _fox_call(fox, frow, batch, seq):
    t = fox.shape[0]
    nq = seq // ATT_T
    qkv_specs, out_spec, scratch = _att_specs(seq)
    return pl.pallas_call(
        _fox_kernel,
        grid=(batch, nq),
        in_specs=qkv_specs + [pl.BlockSpec((1, nq, 8, ATT_T), lambda b, i: (b, 0, 0, 0))],
        out_specs=out_spec,
        out_shape=jax.ShapeDtypeStruct((t, GROUP), BF16),
        scratch_shapes=scratch,
        compiler_params=_cparams(("arbitrary", "arbitrary")),
    )(fox, fox, fox, frow)


def _diff_kernel(lam_ref, q_ref, k_ref, v_ref, g_ref, o_ref, k_s, v_s, *, out_scale):
    qi = pl.program_id(1)

    @pl.when(qi == 0)
    def _():
        _pad_kv(k_ref, v_ref, k_s, v_s)

    lam = lam_ref[0]
    lane = lax.broadcasted_iota(jnp.int32, (ATT_T, LANES), 1)
    outs = []
    for pair in range(_HEADS // 2):
        qs, slots = [], []
        for h in (2 * pair, 2 * pair + 1):
            q = _slot(q_ref[:, h * _HEAD_W:(h + 1) * _HEAD_W])
            qs += [jnp.where(lane < DIFF_D, q, jnp.zeros_like(q)),
                   jnp.where(lane >= DIFF_D, q, jnp.zeros_like(q))]
            slots += [h, h]
        accs = _attend(qs, slots, (k_s, v_s), qi, lambda j, kb: None)
        for j in (0, 2):
            y = (accs[j][:, 0:_HEAD_W] / accs[j][:, _HEAD_W:_HEAD_W + 1]
                 - lam * (accs[j + 1][:, 0:_HEAD_W] / accs[j + 1][:, _HEAD_W:_HEAD_W + 1]))
            ms = jnp.mean(y * y, axis=-1, keepdims=True)
            outs.append(y * lax.rsqrt(ms + EPS) * g_ref[...] * out_scale)
    o_ref[...] = jnp.concatenate(outs, axis=1).astype(BF16)


def _diff_call(diff, lam, subln_g, out_scale, batch, seq):
    t = diff.shape[0]
    nq = seq // ATT_T
    qkv_specs, out_spec, scratch = _att_specs(seq)
    return pl.pallas_call(
        functools.partial(_diff_kernel, out_scale=out_scale),
        grid=(batch, nq),
        in_specs=([pl.BlockSpec(memory_space=pltpu.SMEM)] + qkv_specs
                  + [pl.BlockSpec((1, DIFF_DV), lambda b, i: (0, 0))]),
        out_specs=out_spec,
        out_shape=jax.ShapeDtypeStruct((t, GROUP), BF16),
        scratch_shapes=scratch,
        compiler_params=_cparams(("arbitrary", "arbitrary")),
    )(lam, diff, diff, diff, subln_g)


_HG_LEVELS = (32, 16, 8)


def _hg_ref_rows(G, level):
    C = G.shape[0]
    rows = []
    for tile in range(C // 8):
        if level is None:
            r = tile * 8 + 3
        else:
            grp = (tile * 8) // (2 * level)
            r = grp * 2 * level + level - 1
        rows.append(jnp.broadcast_to(G[r:r + 1, :], (8, G.shape[1])))
    return jnp.concatenate(rows, axis=0)


def _hgrn2_kernel(p_ref, lb_ref, ng_ref, o_ref, state_ref):
    i = pl.program_id(1)
    tm = p_ref.shape[0]
    C = HG_CHUNK
    KW = HG_HEADS * HG_DK

    @pl.when(i == 0)
    def _():
        state_ref[...] = jnp.zeros_like(state_ref)

    lb = lb_ref[...]
    rr = lax.broadcasted_iota(jnp.int32, (C, C), 0)
    cc = lax.broadcasted_iota(jnp.int32, (C, C), 1)
    tri = jnp.where(cc <= rr, 1.0, 0.0).astype(BF16)
    level_masks = []
    for m in _HG_LEVELS:
        same_grp = _shr(rr, 2 * m) == _shr(cc, 2 * m)
        level_masks.append(same_grp & ((rr & (2 * m - 1)) >= m) & ((cc & (2 * m - 1)) < m))
    diag_mask = (_shr(rr, 8) == _shr(cc, 8)) & (cc <= rr)
    row_idx = lax.broadcasted_iota(jnp.int32, (C, HG_DK), 0)

    def chunk(base, states):
        hq = p_ref[pl.ds(base, C), 0:KW].astype(F32)
        q_all = hq * _sigmoid(hq)
        f = lb + (1.0 - lb) * _sigmoid(p_ref[pl.ds(base, C), KW:2 * KW].astype(F32))
        k_all = 1.0 - f
        g_all = _split_dot(tri, jnp.log(f), terms=2)
        new_states = []
        for h in range(HG_HEADS):
            ks = slice(h * HG_DK, (h + 1) * HG_DK)
            q = q_all[:, ks]
            k = k_all[:, ks]
            G = g_all[:, ks]
            v = p_ref[pl.ds(base, C), 2 * KW + h * HG_DV:2 * KW + (h + 1) * HG_DV]
            st = states[h]
            gl = G[C - 1:C, :]
            o = _dot_nt((q * jnp.exp(G)).astype(BF16), st.astype(BF16))
            a = jnp.zeros((C, C), F32)
            for m, mask in zip(_HG_LEVELS, level_masks):
                ref = _hg_ref_rows(G, m)
                upper = (row_idx & (2 * m - 1)) >= m
                e = jnp.exp(jnp.where(upper, G - ref, ref - G))
                am = _dot_nt((q * e).astype(BF16), (k * e).astype(BF16))
                a = a + jnp.where(mask, am, 0.0)
            ref = _hg_ref_rows(G, None)
            d = jnp.clip(G - ref, -80.0, 80.0)
            am = _dot_nt((q * jnp.exp(d)).astype(BF16), (k * jnp.exp(-d)).astype(BF16))
            a = a + jnp.where(diag_mask, am, 0.0)
            o = o + _dot(a.astype(BF16), v)
            kd = (k * jnp.exp(gl - G)).astype(BF16)
            new_states.append(jnp.exp(gl) * st + _dot_tn(v, kd))
            ms = jnp.mean(o * o, axis=-1, keepdims=True)
            on = o * lax.rsqrt(ms + EPS) * ng_ref[...]
            gate = p_ref[pl.ds(base, C), 2 * KW + GROUP + h * HG_DV:
                         2 * KW + GROUP + (h + 1) * HG_DV].astype(F32)
            o_ref[pl.ds(base, C), h * HG_DV:(h + 1) * HG_DV] = (
                on * (gate * _sigmoid(gate))).astype(BF16)
        return tuple(new_states)

    def pair(ci, states):
        first = pl.multiple_of(ci * 2 * C, 2 * C)
        second = pl.multiple_of(ci * 2 * C + C, C)
        return chunk(second, chunk(first, states))

    states = lax.fori_loop(0, tm // (2 * C), pair,
                           tuple(state_ref[h] for h in range(HG_HEADS)))
    for h in range(HG_HEADS):
        state_ref[h] = states[h]


def _hgrn2_call(hg, lb, norm_g, batch, seq):
    t = hg.shape[0]
    tm = min(ROW_TILE, seq)
    nt = seq // tm
    kw = HG_HEADS * HG_DK
    return pl.pallas_call(
        _hgrn2_kernel,
        grid=(batch, nt),
        in_specs=[pl.BlockSpec((tm, HG_COLS), lambda b, i: (b * nt + i, 0)),
                  pl.BlockSpec((1, kw), lambda b, i: (0, 0)),
                  pl.BlockSpec((1, HG_DV), lambda b, i: (0, 0))],
        out_specs=pl.BlockSpec((tm, GROUP), lambda b, i: (b * nt + i, 0)),
        out_shape=jax.ShapeDtypeStruct((t, GROUP), BF16),
        scratch_shapes=[pltpu.VMEM((HG_HEADS, HG_DV, HG_DK), F32)],
        compiler_params=_cparams(("arbitrary", "arbitrary")),
    )(hg, lb, norm_g)


def _s5_kernel(u_ref, bb_ref, ar_ref, ai_ref, cm_ref, d_ref, glu_ref, o_ref, bu_s, xr_s, xi_s,
               *, batch):
    i = pl.program_id(0)
    nst = S5_GROUPS * S5_N
    steps = u_ref.shape[0] // batch

    @pl.when(i == 0)
    def _():
        xr_s[...] = jnp.zeros_like(xr_s)
        xi_s[...] = jnp.zeros_like(xi_s)

    u = u_ref[...]
    bu_s[...] = _dot(u, bb_ref[...])
    ar = jnp.broadcast_to(ar_ref[...], (batch, nst))
    ai = jnp.broadcast_to(ai_ref[...], (batch, nst))

    def step(t, carry):
        xr, xi = carry
        r0 = pl.multiple_of(t * batch, batch)
        nr = ar * xr - ai * xi + bu_s[pl.ds(r0, batch), 0:nst]
        ni = ar * xi + ai * xr + bu_s[pl.ds(r0, batch), nst:2 * nst]
        bu_s[pl.ds(r0, batch), 0:nst] = nr
        bu_s[pl.ds(r0, batch), nst:2 * nst] = ni
        return nr, ni

    xr, xi = lax.fori_loop(0, steps, step, (xr_s[...], xi_s[...]))
    xr_s[...] = xr
    xi_s[...] = xi
    y = _dot(bu_s[...].astype(BF16), cm_ref[...]) + d_ref[...] * u.astype(F32)
    gel = 0.5 * y * (1.0 + jnp.tanh(0.7978845608028654 * (y + 0.044715 * (y * y * y))))
    z = _dot(gel.astype(BF16), glu_ref[...])
    o_ref[...] = (z[:, 0:GROUP] * _sigmoid(z[:, GROUP:2 * GROUP])).astype(BF16)


def _s5_call(u_sb, bbar, ar, ai, cmat, dskip, glu_w, batch, seq):
    rows = u_sb.shape[0]
    steps = min(S5_STEPS, seq)
    tr = steps * batch
    nst = S5_GROUPS * S5_N
    const = lambda i: (0, 0)
    return pl.pallas_call(
        functools.partial(_s5_kernel, batch=batch),
        grid=(seq // steps,),
        in_specs=[pl.BlockSpec((tr, GROUP), lambda i: (i, 0)),
                  pl.BlockSpec((GROUP, 2 * nst), const),
                  pl.BlockSpec((1, nst), const),
                  pl.BlockSpec((1, nst), const),
                  pl.BlockSpec((2 * nst, GROUP), const),
                  pl.BlockSpec((1, GROUP), const),
                  pl.BlockSpec((GROUP, 2 * GROUP), const)],
        out_specs=pl.BlockSpec((tr, GROUP), lambda i: (i, 0)),
        out_shape=jax.ShapeDtypeStruct((rows, GROUP), BF16),
        scratch_shapes=[pltpu.VMEM((tr, 2 * nst), F32),
                        pltpu.VMEM((batch, nst), F32), pltpu.VMEM((batch, nst), F32)],
        compiler_params=_cparams(("arbitrary",)),
    )(u_sb, bbar, ar, ai, cmat, dskip, glu_w)


def _s5_params(a_re, a_im, log_step, b_re, b_im, c_re, c_im):
    lr = jnp.minimum(a_re.astype(F32), -1e-4)
    li = a_im.astype(F32)
    dt = jnp.exp(log_step.astype(F32))[:, None]
    mag = jnp.exp(lr * dt)
    ar = mag * jnp.cos(li * dt)
    ai = mag * jnp.sin(li * dt)
    den = lr * lr + li * li
    zr = ((ar - 1.0) * lr + ai * li) / den
    zi = (ai * lr - (ar - 1.0) * li) / den
    br = b_re.astype(F32)
    bi = b_im.astype(F32)
    bbr = zr[..., None] * br - zi[..., None] * bi
    bbi = zr[..., None] * bi + zi[..., None] * br
    eye = jnp.eye(S5_GROUPS, dtype=F32)
    def bd_in(m):
        return jnp.einsum('gnh,gk->ghkn', m, eye).reshape(GROUP, S5_GROUPS * S5_N)
    def bd_out(m):
        return jnp.einsum('ghn,gk->gnkh', m, eye).reshape(S5_GROUPS * S5_N, GROUP)
    bbar = jnp.concatenate([bd_in(bbr), bd_in(bbi)], axis=1).astype(BF16)
    cmat = jnp.concatenate([bd_out(c_re.astype(F32)), -bd_out(c_im.astype(F32))],
                           axis=0).astype(BF16)
    return bbar, ar.reshape(1, -1), ai.reshape(1, -1), cmat


def _outproj_kernel(yf_ref, yh_ref, yd_ref, ys_ref, wo_ref, x_ref, pg_ref, g1_ref, g2_ref,
                    sc_ref, sh_ref, rw_ref, rb_ref,
                    x1_ref, h2_ref, idx_ref, wt_ref, cnt_ref, carry_ref):
    tm = x_ref.shape[0]

    @pl.when((pl.program_id(0) == 0) & (pl.program_id(1) == 0))
    def _():
        carry_ref[...] = jnp.zeros_like(carry_ref)

    y = (_dot(yf_ref[...], wo_ref[0:GROUP, :])
         + _dot(yh_ref[...], wo_ref[GROUP:2 * GROUP, :])
         + _dot(yd_ref[...], wo_ref[2 * GROUP:3 * GROUP, :])
         + _dot(ys_ref[...], wo_ref[3 * GROUP:4 * GROUP, :]))
    ms = jnp.mean(y * y, axis=-1, keepdims=True)
    x1 = x_ref[...] + g1_ref[0] * (y * lax.rsqrt(ms + EPS) * pg_ref[...])
    x1_ref[...] = x1
    ms2 = jnp.mean(x1 * x1, axis=-1, keepdims=True)
    h2 = (x1 * lax.rsqrt(ms2 + EPS) * g2_ref[...]) * (1.0 + sc_ref[0]) + sh_ref[0]
    for c in range(ROW_SPLIT):
        h2_ref[pl.ds(c, tm, stride=ROW_SPLIT), :] = h2[:, c * LANES:(c + 1) * LANES]
    h_hi = h2.astype(BF16)
    h_lo = (h2 - h_hi.astype(F32)).astype(BF16)
    rw = rw_ref[...]
    w_hi = rw.astype(BF16)
    w_lo = (rw - w_hi.astype(F32)).astype(BF16)
    hh = _dot(h_hi, jnp.concatenate([w_hi, w_lo], axis=1))
    logits = hh[:, 0:LANES] + (_dot(h_lo, w_hi) + hh[:, LANES:2 * LANES]) + rb_ref[...]
    lane = lax.broadcasted_iota(jnp.int32, (tm, LANES), 1)
    cur = logits
    vals, idxs = [], []
    for _ in range(TOP_K):
        m = jnp.max(cur, axis=-1, keepdims=True)
        sel = jnp.min(jnp.where(cur == m, lane, LANES), axis=-1, keepdims=True)
        vals.append(m)
        idxs.append(sel)
        cur = jnp.where(lane == sel, -jnp.inf, cur)
    es = [jnp.exp(v - vals[0]) for v in vals]
    tot = es[0] + es[1] + es[2] + es[3]
    hits = (lane == idxs[0]) | (lane == idxs[1]) | (lane == idxs[2]) | (lane == idxs[3])
    cnt = jnp.where(hits, 1.0, 0.0)
    r = lax.broadcasted_iota(jnp.int32, (tm, tm), 0)
    cidx = lax.broadcasted_iota(jnp.int32, (tm, tm), 1)
    below = jnp.where(cidx < r, 1.0, 0.0).astype(BF16)
    before = _dot(below, cnt.astype(BF16)) + carry_ref[...]
    total = carry_ref[...] + jnp.sum(cnt, axis=0, keepdims=True)
    carry_ref[...] = total
    cnt_ref[...] = total
    idx_out = jnp.zeros((tm, LANES), jnp.int32)
    wt_out = jnp.zeros((tm, LANES), F32)
    for k in range(TOP_K):
        rank = jnp.sum(jnp.where(lane == idxs[k], before, 0.0), axis=-1, keepdims=True)
        idx_out = jnp.where(lane == k, idxs[k], idx_out)
        idx_out = jnp.where(lane == TOP_K + k, rank.astype(jnp.int32), idx_out)
        wt_out = jnp.where(lane == k, es[k] / tot, wt_out)
    idx_ref[...] = idx_out
    wt_ref[...] = wt_out


def _outproj_call(yf, yh, yd, ys, w_out, x2, post_g, g1, pre_g2, sc2, sh2, rw, rb, batch, seq):
    t = x2.shape[0]
    tm = min(ROW_TILE, seq)
    nt = seq // tm
    row = lambda b, i: (b * nt + i, 0)
    const2 = lambda b, i: (0, 0)
    per_b = lambda b, i: (b, 0, 0)
    grp = pl.BlockSpec((tm, GROUP), row)
    vec = pl.BlockSpec((1, D_MODEL), const2)
    mod = pl.BlockSpec((1, 1, D_MODEL), per_b)
    return pl.pallas_call(
        _outproj_kernel,
        grid=(batch, nt),
        in_specs=[grp, grp, grp, grp,
                  pl.BlockSpec((D_MODEL, D_MODEL), const2),
                  pl.BlockSpec((tm, D_MODEL), row),
                  vec, mod, vec, mod, mod,
                  pl.BlockSpec((D_MODEL, LANES), const2),
                  pl.BlockSpec((1, LANES), const2)],
        out_specs=[pl.BlockSpec((tm, D_MODEL), row),
                   pl.BlockSpec((tm * ROW_SPLIT, LANES), row),
                   pl.BlockSpec((tm, LANES), row), pl.BlockSpec((tm, LANES), row),
                   pl.BlockSpec((1, LANES), const2)],
        out_shape=[jax.ShapeDtypeStruct((t, D_MODEL), F32),
                   jax.ShapeDtypeStruct((t * ROW_SPLIT, LANES), F32),
                   jax.ShapeDtypeStruct((t, LANES), jnp.int32),
                   jax.ShapeDtypeStruct((t, LANES), F32),
                   jax.ShapeDtypeStruct((1, LANES), F32)],
        scratch_shapes=[pltpu.VMEM((1, LANES), F32)],
        compiler_params=_cparams(("arbitrary", "arbitrary")),
    )(yf, yh, yd, ys, w_out, x2, post_g, g1, pre_g2, sc2, sh2, rw, rb)


def _row_copy(src_hbm, dst_vmem, src_row, dst_row, sem):
    s0 = pl.multiple_of(src_row * ROW_SPLIT, ROW_SPLIT)
    return pltpu.make_async_copy(src_hbm.at[pl.ds(s0, ROW_SPLIT), :],
                                 dst_vmem.at[pl.ds(dst_row * ROW_SPLIT, ROW_SPLIT), :], sem)


def _start_rows(idx_ref, n, src_hbm, dst_vmem, sem, idx_off=0, both_priorities=False):
    for j in range(n):
        _row_copy(src_hbm, dst_vmem, idx_ref[0, 0, idx_off + j], j, sem).start(
            priority=j % 2 if both_priorities else 0)


def _wait_rows(n, src_hbm, dst_vmem, sem):
    pltpu.make_async_copy(src_hbm.at[pl.ds(0, n * ROW_SPLIT), :], dst_vmem, sem).wait()


def _tile_rows(buf, n):
    return jnp.concatenate([buf[pl.ds(c, n, stride=ROW_SPLIT), :] for c in range(ROW_SPLIT)],
                           axis=1)


def _expert_kernel(te_ref, src_ref, nxt_ref, h_hbm, w1_ref, p_ref, b1_ref, w2_ref, b2_ref,
                   o_ref, xa, xb, w1_s, w2_s, sem):
    i = pl.program_id(0)
    last = pl.num_programs(0) - 1
    tm = EXP_TILE

    @pl.when(i == 0)
    def _():
        _start_rows(src_ref, tm, h_hbm, xa, sem.at[0])

    @pl.when((i == 0) | (te_ref[i] != te_ref[jnp.maximum(i - 1, 0)]))
    def _():
        chunk = 2 * LANES
        for c in range(2 * D_FF // chunk):
            r = _dot(w1_ref[0, 0, :, c * chunk:(c + 1) * chunk].astype(BF16), p_ref[...])
            w1_s[:, c * LANES:(c + 1) * LANES] = r[:, 0:LANES].astype(BF16)
            w1_s[:, D_FF + c * LANES:D_FF + (c + 1) * LANES] = r[:, LANES:chunk].astype(BF16)
        w2_s[...] = w2_ref[0, 0].astype(BF16)

    def step(cur, cur_sem, nxt, nxt_sem):
        _wait_rows(tm, h_hbm, cur, cur_sem)
        _start_rows(nxt_ref, tm, h_hbm, nxt, nxt_sem)
        hh = _dot(_tile_rows(cur, tm).astype(BF16), w1_s[...]) + b1_ref[0]
        glu = jnp.minimum(hh[:, 0:D_FF], SWIGLU_LIMIT)
        lin = jnp.clip(hh[:, D_FF:2 * D_FF], -SWIGLU_LIMIT, SWIGLU_LIMIT)
        act = glu * _sigmoid(SWIGLU_ALPHA * glu) * (lin + 1.0)
        y = _dot(act.astype(BF16), w2_s[...]) + b2_ref[0]
        for c in range(ROW_SPLIT):
            o_ref[pl.ds(c, tm, stride=ROW_SPLIT), :] = y[:, c * LANES:(c + 1) * LANES]

        @pl.when(i == last)
        def _():
            _wait_rows(tm, h_hbm, nxt, nxt_sem)

    @pl.when(lax.rem(i, 2) == 0)
    def _():
        step(xa, sem.at[0], xb, sem.at[1])

    @pl.when(lax.rem(i, 2) == 1)
    def _():
        step(xb, sem.at[1], xa, sem.at[0])


def _expert_call(tile_expert, src_rows, h2, w1, b1, w2, b2, layer):
    ntiles = tile_expert.shape[0]
    tm = EXP_TILE
    src3 = src_rows.reshape(ntiles, 1, tm)
    perm = np.zeros((2 * LANES, 2 * LANES), np.float32)
    perm[2 * np.arange(LANES), np.arange(LANES)] = 1.0
    perm[2 * np.arange(LANES) + 1, LANES + np.arange(LANES)] = 1.0
    grid_spec = pltpu.PrefetchScalarGridSpec(
        num_scalar_prefetch=1,
        grid=(ntiles,),
        in_specs=[pl.BlockSpec((1, 1, tm), lambda i, te: (i, 0, 0), memory_space=pltpu.SMEM),
                  pl.BlockSpec((1, 1, tm), lambda i, te: (jnp.minimum(i + 1, ntiles - 1), 0, 0),
                               memory_space=pltpu.SMEM),
                  pl.BlockSpec(memory_space=pl.ANY),
                  pl.BlockSpec((1, 1, D_MODEL, 2 * D_FF), lambda i, te: (layer, te[i], 0, 0)),
                  pl.BlockSpec((2 * LANES, 2 * LANES), lambda i, te: (0, 0)),
                  pl.BlockSpec((1, 1, 2 * D_FF), lambda i, te: (te[i], 0, 0)),
                  pl.BlockSpec((1, 1, D_FF, D_MODEL), lambda i, te: (layer, te[i], 0, 0)),
                  pl.BlockSpec((1, 1, D_MODEL), lambda i, te: (te[i], 0, 0))],
        out_specs=pl.BlockSpec((tm * ROW_SPLIT, LANES), lambda i, te: (i, 0)),
        scratch_shapes=[pltpu.VMEM((tm * ROW_SPLIT, LANES), F32),
                        pltpu.VMEM((tm * ROW_SPLIT, LANES), F32),
                        pltpu.VMEM((D_MODEL, 2 * D_FF), BF16),
                        pltpu.VMEM((D_FF, D_MODEL), BF16),
                        pltpu.SemaphoreType.DMA((2,))],
    )
    return pl.pallas_call(
        _expert_kernel,
        grid_spec=grid_spec,
        out_shape=jax.ShapeDtypeStruct((ntiles * tm * ROW_SPLIT, LANES), F32),
        compiler_params=_cparams(("arbitrary",)),
    )(tile_expert, src3, src3, h2, w1, jnp.asarray(perm, BF16), b1, w2, b2)


def _combine_kernel(pos_ref, nxt_ref, y_hbm, wt_ref, x_ref, pg_ref, g_ref, o_ref, ya, yb, sem):
    i = pl.program_id(0) * pl.num_programs(1) + pl.program_id(1)
    last = pl.num_programs(0) * pl.num_programs(1) - 1
    tm = x_ref.shape[0]

    def start(idx_ref, buf, s):
        for k in range(TOP_K):
            _start_rows(idx_ref, tm, y_hbm, buf.at[k], s, idx_off=k * tm, both_priorities=True)

    def wait(buf, s):
        for k in range(TOP_K):
            _wait_rows(tm, y_hbm, buf.at[k], s)

    @pl.when(i == 0)
    def _():
        start(pos_ref, ya, sem.at[0])

    def step(cur, cur_sem, nxt, nxt_sem):
        wait(cur, cur_sem)
        start(nxt_ref, nxt, nxt_sem)
        wt = wt_ref[...]
        y = wt[:, 0:1] * _tile_rows(cur.at[0], tm)
        for k in range(1, TOP_K):
            y = y + wt[:, k:k + 1] * _tile_rows(cur.at[k], tm)
        ms = jnp.mean(y * y, axis=-1, keepdims=True)
        o_ref[...] = x_ref[...] + g_ref[0] * (y * lax.rsqrt(ms + EPS) * pg_ref[...])

        @pl.when(i == last)
        def _():
            wait(nxt, nxt_sem)

    @pl.when(lax.rem(i, 2) == 0)
    def _():
        step(ya, sem.at[0], yb, sem.at[1])

    @pl.when(lax.rem(i, 2) == 1)
    def _():
        step(yb, sem.at[1], ya, sem.at[0])


def _combine_call(dest, y_sorted, wts, x1, post_g, g2, batch, seq):
    t = x1.shape[0]
    tm = min(CMB_TILE, seq)
    nt = seq // tm
    ntiles = t // tm
    dest_t = dest.reshape(ntiles, tm, TOP_K).transpose(0, 2, 1).reshape(ntiles, 1, TOP_K * tm)
    row = lambda b, i: (b * nt + i, 0)
    return pl.pallas_call(
        _combine_kernel,
        grid=(batch, nt),
        in_specs=[pl.BlockSpec((1, 1, TOP_K * tm), lambda b, i: (b * nt + i, 0, 0),
                               memory_space=pltpu.SMEM),
                  pl.BlockSpec((1, 1, TOP_K * tm),
                               lambda b, i: (jnp.minimum(b * nt + i + 1, ntiles - 1), 0, 0),
                               memory_space=pltpu.SMEM),
                  pl.BlockSpec(memory_space=pl.ANY),
                  pl.BlockSpec((tm, LANES), row),
                  pl.BlockSpec((tm, D_MODEL), row),
                  pl.BlockSpec((1, D_MODEL), lambda b, i: (0, 0)),
                  pl.BlockSpec((1, 1, D_MODEL), lambda b, i: (b, 0, 0))],
        out_specs=pl.BlockSpec((tm, D_MODEL), row),
        out_shape=jax.ShapeDtypeStruct((t, D_MODEL), F32),
        scratch_shapes=[pltpu.VMEM((TOP_K, tm * ROW_SPLIT, LANES), F32),
                        pltpu.VMEM((TOP_K, tm * ROW_SPLIT, LANES), F32),
                        pltpu.SemaphoreType.DMA((2,))],
        compiler_params=_cparams(("arbitrary", "arbitrary")),
    )(dest_t, dest_t, y_sorted, wts, x1, post_g, g2)


def _dispatch_plan(idx, rank, counts, t):
    tm = EXP_TILE
    npairs = t * TOP_K
    ntiles = npairs // tm + N_EXPERTS
    e_flat = idx.reshape(-1)
    pair = jnp.arange(npairs, dtype=jnp.int32)
    _, order = lax.sort((e_flat, pair), num_keys=1, is_stable=True)
    cnt_excl = jnp.cumsum(counts) - counts
    tiles_per = (counts + tm - 1) // tm
    tile_end = jnp.cumsum(tiles_per)
    tile_start = tile_end - tiles_per
    tile_ids = jnp.arange(ntiles, dtype=jnp.int32)
    tile_expert = jnp.minimum(
        jnp.sum((tile_ids[:, None] >= tile_end[None, :]).astype(jnp.int32), axis=1),
        N_EXPERTS - 1).astype(jnp.int32)
    t_in = tile_ids - tile_start[tile_expert]
    base = cnt_excl[tile_expert] + t_in * tm
    room = jnp.where(tile_ids < tile_end[-1], counts[tile_expert] - t_in * tm, 0)
    j = jnp.arange(tm, dtype=jnp.int32)
    valid = j[None, :] < room[:, None]
    sidx = jnp.clip(base[:, None] + j[None, :], 0, npairs - 1)
    src_rows = jnp.where(valid, order[sidx] // TOP_K, 0).astype(jnp.int32)
    dest = (tile_start[idx] * tm + rank).astype(jnp.int32)
    return tile_expert, src_rows, dest


def _rope_tables():
    lane = np.arange(GROUP)
    d = lane % DIFF_D
    rd = DIFF_D // 4
    half = rd // 2
    inv = np.where(d < rd, ROPE_THETA ** (-(d % half).astype(np.float32) / half), 0.0)
    ma = np.where(d < half, -1.0, 0.0)
    mb = np.where((d >= half) & (d < rd), 1.0, 0.0)
    f = lambda a: jnp.asarray(a.reshape(1, GROUP), F32)
    return f(inv), f(ma), f(mb)


def kernel(x, c, positions, ada_w, ada_b, pre_norm_g, post_norm_g, w_in, w_out, fox_fb, hg_lower, hg_norm_g, diff_lam_q1, diff_lam_k1, diff_lam_q2, diff_lam_k2, diff_subln_g, s5_a_re, s5_a_im, s5_log_step, s5_b_re, s5_b_im, s5_c_re, s5_c_im, s5_d, s5_glu_w, router_w, router_b, exp_w1, exp_b1, exp_w2, exp_b2):
    batch, seq, d = x.shape
    depth = ada_w.shape[0]
    t = batch * seq
    assert d == D_MODEL and seq % ROW_TILE == 0

    lb_all = jnp.cumsum(jax.nn.softmax(hg_lower.astype(F32), axis=0), axis=0)
    lb_all = lb_all - lb_all[0:1]
    mod = _ada_call(c, ada_w, ada_b)
    inv_lane, mask_a, mask_b = _rope_tables()
    pos = positions.reshape(t, 1).astype(jnp.int32)
    xf = x.reshape(t, d)

    w_main, w_f = _win_prep_call(w_in)
    for l in range(depth):
        m6 = mod[l].reshape(batch, 6, 1, d)
        sh1, sc1, g1, sh2, sc2, g2 = [m6[:, j] for j in range(6)]
        fb = jnp.pad(fox_fb[l], (0, LANES - FOX_HEADS)).reshape(1, LANES)
        fox, frow, hg, diff, s5u = _inproj_call(
            xf, pre_norm_g[l, 0].reshape(1, d), sc1, sh1, w_main, w_f, fb, pos,
            inv_lane, mask_a, mask_b, batch, seq, l)

        y_fox = _fox_call(fox, frow, batch, seq)

        lam_init = 0.8 - 0.6 * math.exp(-0.3 * l)
        lam = (jnp.exp(jnp.sum(diff_lam_q1[l].astype(F32) * diff_lam_k1[l].astype(F32)))
               - jnp.exp(jnp.sum(diff_lam_q2[l].astype(F32) * diff_lam_k2[l].astype(F32))) + lam_init)
        y_diff = _diff_call(diff, lam.reshape(1), diff_subln_g[l].reshape(1, DIFF_DV),
                            1.0 - lam_init, batch, seq)

        y_hg = _hgrn2_call(hg, lb_all[l].reshape(1, -1), hg_norm_g[l].reshape(1, HG_DV), batch, seq)

        bbar, ar, ai, cmat = _s5_params(s5_a_re[l], s5_a_im[l], s5_log_step[l], s5_b_re[l],
                                        s5_b_im[l], s5_c_re[l], s5_c_im[l])
        u_sb = s5u.reshape(batch, seq, GROUP).transpose(1, 0, 2).reshape(t, GROUP)
        y_s5_sb = _s5_call(u_sb, bbar, ar, ai, cmat, s5_d[l].reshape(1, GROUP),
                           s5_glu_w[l].astype(BF16), batch, seq)
        y_s5 = y_s5_sb.reshape(seq, batch, GROUP).transpose(1, 0, 2).reshape(t, GROUP)

        rw = jnp.pad(router_w[l], ((0, 0), (0, LANES - N_EXPERTS)))
        rb = jnp.pad(router_b[l], (0, LANES - N_EXPERTS), constant_values=NEG_BIG).reshape(1, LANES)
        x1, h2, idx, wts, cnt = _outproj_call(
            y_fox, y_hg, y_diff, y_s5, w_out[l].astype(BF16), xf,
            post_norm_g[l, 0].reshape(1, d), g1, pre_norm_g[l, 1].reshape(1, d), sc2, sh2,
            rw, rb, batch, seq)

        tile_expert, src_rows, dest = _dispatch_plan(
            idx[:, 0:TOP_K], idx[:, TOP_K:2 * TOP_K], cnt[0, 0:N_EXPERTS].astype(jnp.int32), t)
        b1 = exp_b1[l]
        b1p = jnp.concatenate([b1[:, 0::2], b1[:, 1::2]], axis=-1).reshape(N_EXPERTS, 1, 2 * D_FF)
        y_sorted = _expert_call(tile_expert, src_rows, h2, exp_w1, b1p, exp_w2,
                                exp_b2[l].reshape(N_EXPERTS, 1, D_MODEL), l)
        xf = _combine_call(dest, y_sorted, wts, x1, post_norm_g[l, 1].reshape(1, d), g2, batch, seq)

    return xf.reshape(batch, seq, d)
```
